```python
import math
import jax, jax.numpy as jnp
from jax import lax
import numpy as np

D_MODEL = 1024
BATCH = 4
SEQ = 4096
DEPTH = 2
DEC_BATCH = 32
DEC_SEQ = 4
PAST_LEN = 8192
PAGE_SIZE = 128

SSD_D_INNER = D_MODEL
SSD_HEAD_DIM = 64
SSD_N_HEADS = SSD_D_INNER // SSD_HEAD_DIM
SSD_N_GROUPS = 4
SSD_HEADS_PER_GROUP = SSD_N_HEADS // SSD_N_GROUPS
SSD_D_STATE = 128
SSD_CONV = 4
SSD_CHUNK = 128
SSD_CONV_DIM = SSD_D_INNER + 2 * SSD_N_GROUPS * SSD_D_STATE
MOBA_N_HEADS = 8
MOBA_HEAD_DIM = 128
MOBA_WIDTH = MOBA_N_HEADS * MOBA_HEAD_DIM
MOBA_BLOCK = 256
MOBA_TOPK = 3
MOBA_Q_BLOCK = 16
LRU_WIDTH = D_MODEL
LRU_N_BLOCKS = 16
LRU_BLOCK = LRU_WIDTH // LRU_N_BLOCKS
LRU_CONV = 4
LRU_C = 8.0
N_BRANCH = 3
MLP_HIDDEN = 4 * D_MODEL
RMS_EPS = 1e-6

IN_SPLITS = (SSD_D_INNER, SSD_CONV_DIM, SSD_N_HEADS, MOBA_WIDTH, MOBA_WIDTH, MOBA_WIDTH,
             LRU_WIDTH, LRU_WIDTH, N_BRANCH * D_MODEL)
IN_OFFSETS = tuple(int(s) for s in np.cumsum(IN_SPLITS)[:-1])
N_IN = int(sum(IN_SPLITS))

kernel_name = "hybrid_ssd_moba_rglru_decode_step"

F32 = jnp.float32


def rmsnorm(x, g):
    xf = x.astype(F32)
    y = xf * lax.rsqrt(jnp.mean(xf * xf, axis=-1, keepdims=True) + RMS_EPS)
    return (y * g.astype(F32)).astype(x.dtype)


def causal_conv(x, buf, w, b):
    W = w.shape[0]
    T = x.shape[1]
    xp = jnp.concatenate([buf.astype(x.dtype), x], axis=1)
    y = b
    for tap in range(W):
        y = y + w[tap] * xp[:, tap:tap + T]
    return y, xp[:, T:]


def ssd_mixer(z, xbc, dt_raw, conv_buf, h0, conv_w, conv_b, dt_bias, a_log, d_skip, norm_g):
    Bsz, T, _ = z.shape
    G, R, P, N = SSD_N_GROUPS, SSD_HEADS_PER_GROUP, SSD_HEAD_DIM, SSD_D_STATE
    xbc, new_buf = causal_conv(xbc, conv_buf, conv_w, conv_b)
    xbc = jax.nn.silu(xbc.astype(F32))
    xs, bm, cm = jnp.split(xbc, [SSD_D_INNER, SSD_D_INNER + G * N], axis=-1)
    xs = xs.reshape(Bsz, T, G, R, P)
    bm = bm.reshape(Bsz, T, G, N)
    cm = cm.reshape(Bsz, T, G, N)
    dt = jax.nn.softplus(dt_raw.astype(F32) + dt_bias.astype(F32)).reshape(Bsz, T, G, R)
    a = -jnp.exp(a_log.astype(F32)).reshape(G, R)
    L = SSD_CHUNK if T % SSD_CHUNK == 0 else T
    nc = T // L

    def chunk(t):
        return t.reshape((Bsz, nc, L) + t.shape[2:])

    xc, bc, cc, dtc = chunk(xs), chunk(bm), chunk(cm), chunk(dt)
    cum = jnp.cumsum(dtc * a, axis=2)
    causal = jnp.tril(jnp.ones((L, L), dtype=bool))
    seg = cum[:, :, :, None] - cum[:, :, None, :]
    decay = jnp.exp(jnp.where(causal[None, None, :, :, None, None], seg, -jnp.inf))
    cb = jnp.einsum('bclgn,bcsgn->bclsg', cc, bc)
    m = cb[..., None] * decay * dtc[:, :, None]
    y_diag = jnp.einsum('bclsgr,bcsgrp->bclgrp', m, xc)
    decay_end = jnp.exp(cum[:, :, -1:] - cum)
    states = jnp.einsum('bcsgn,bcsgr,bcsgrp->bcgrpn', bc, decay_end * dtc, xc)
    chunk_decay = jnp.exp(cum[:, :, -1])

    def step(h, inp):
        st, dec = inp
        return h * dec[..., None, None] + st, h

    h_init = h0.astype(F32).reshape(Bsz, G, R, P, N)
    h_last, h_before = lax.scan(step, h_init,
                                (jnp.moveaxis(states, 1, 0), jnp.moveaxis(chunk_decay, 1, 0)))
    h_before = jnp.moveaxis(h_before, 0, 1)
    y_off = jnp.einsum('bclgn,bclgr,bcgrpn->bclgrp', cc, jnp.exp(cum), h_before)
    y = y_diag + y_off + d_skip.astype(F32).reshape(G, R)[:, :, None] * xc
    y = y.reshape(Bsz, T, SSD_D_INNER) * jax.nn.silu(z.astype(F32))
    yg = y.reshape(Bsz, T, G, SSD_D_INNER // G)
    yg = yg * lax.rsqrt(jnp.mean(yg * yg, axis=-1, keepdims=True) + RMS_EPS)
    y = yg.reshape(Bsz, T, SSD_D_INNER) * norm_g.astype(F32)
    return y.astype(z.dtype), new_buf, h_last.reshape(Bsz, SSD_N_HEADS, P, N)


def moba_attend(q, k_all, v_all, q_pos):
    Bsz, Tq, H, Dh = q.shape
    Lk = k_all.shape[1]
    nb = -(-Lk // MOBA_BLOCK)
    pad = nb * MOBA_BLOCK - Lk
    if pad:
        k_all = jnp.pad(k_all, ((0, 0), (0, pad), (0, 0), (0, 0)))
        v_all = jnp.pad(v_all, ((0, 0), (0, pad), (0, 0), (0, 0)))
    kb = k_all.reshape(Bsz, nb, MOBA_BLOCK, H, Dh)
    vb = v_all.reshape(Bsz, nb, MOBA_BLOCK, H, Dh)
    kmean = jnp.mean(kb, axis=2, dtype=F32)
    topk = min(MOBA_TOPK, nb)
    qb = MOBA_Q_BLOCK if Tq % MOBA_Q_BLOCK == 0 else Tq
    nq = Tq // qb
    scale = MOBA_HEAD_DIM ** -0.5
    bi = jnp.arange(Bsz)[:, None, None, None]
    hi = jnp.arange(H)[None, None, :, None]
    offs = jnp.arange(MOBA_BLOCK, dtype=jnp.int32)
    blk_ids = jnp.arange(nb, dtype=jnp.int32)

    def one_block(args):
        qc, pc = args
        qf = qc.astype(F32)
        own = pc // MOBA_BLOCK
        gate = jnp.einsum('bqhd,bnhd->bqhn', qf, kmean)
        past = blk_ids[None, :] < own[:, None]
        gate = jnp.where(past[None, :, None, :], gate, -jnp.inf)
        _, sel = lax.top_k(gate, topk)
        sel_ok = sel < own[None, :, None, None]
        own_b = jnp.broadcast_to(own[None, :, None, None], (Bsz, qb, H, 1)).astype(sel.dtype)
        idx = jnp.concatenate([sel, own_b], axis=-1)
        ok = jnp.concatenate([sel_ok, jnp.ones((Bsz, qb, H, 1), dtype=bool)], axis=-1)
        kg = kb[bi, idx, :, hi]
        vg = vb[bi, idx, :, hi]
        logits = jnp.einsum('bqhd,bqhksd->bqhks', qf, kg.astype(F32)) * scale
        kpos = idx[..., None] * MOBA_BLOCK + offs
        mask = ok[..., None] & (kpos <= pc[None, :, None, None, None])
        logits = jnp.where(mask, logits, -jnp.inf)
        p = jax.nn.softmax(logits.reshape(Bsz, qb, H, -1), axis=-1).reshape(logits.shape)
        out = jnp.einsum('bqhks,bqhksd->bqhd', p, vg.astype(F32))
        return out.astype(q.dtype)

    qs = q.reshape(Bsz, nq, qb, H, Dh).swapaxes(0, 1)
    ps = q_pos.reshape(nq, qb)
    out = lax.map(one_block, (qs, ps))
    return out.swapaxes(0, 1).reshape(Bsz, Tq, H * Dh)


def rglru_branch(xr, gr, conv_buf, h0, conv_w, conv_b, wa, ba, wx, bx, lam):
    Bsz, T, _ = xr.shape
    xc, new_buf = causal_conv(xr, conv_buf, conv_w, conv_b)
    xf = xc.astype(F32)
    xb = xf.reshape(Bsz, T, LRU_N_BLOCKS, LRU_BLOCK)
    r = jax.nn.sigmoid(jnp.einsum('btnd,nde->btne', xb, wa.astype(F32)) + ba).reshape(Bsz, T, LRU_WIDTH)
    i = jax.nn.sigmoid(jnp.einsum('btnd,nde->btne', xb, wx.astype(F32)) + bx).reshape(Bsz, T, LRU_WIDTH)
    log_a = -LRU_C * r * jax.nn.softplus(-lam.astype(F32))
    a = jnp.exp(log_a)
    b = jnp.sqrt(-jnp.expm1(2.0 * log_a)) * (i * xf)
    b = b.at[:, 0].add(a[:, 0] * h0.astype(F32))

    def combine(left, right):
        al, bl = left
        ar, br = right
        return al * ar, ar * bl + br

    _, h = lax.associative_scan(combine, (a, b), axis=1)
    y = h * jax.nn.gelu(gr.astype(F32))
    return y.astype(xr.dtype), new_buf, h[:, -1]


def decoder_layer(x, c, past_k, past_v, q_pos, ssd_buf, ssd_h, lru_buf, lru_h, prm):
    Bsz, T, _ = x.shape
    mod = jnp.einsum('bd,de->be', jax.nn.silu(c), prm['w_mod']) + prm['b_mod']
    sh1, sc1, gt1, sh2, sc2, gt2 = jnp.split(mod[:, None, :], 6, axis=-1)
    h = rmsnorm(x, prm['g_pre_mix']) * (1.0 + sc1) + sh1
    proj = jnp.einsum('btd,de->bte', h, prm['w_in'])
    z, xbc, dt_raw, q, k, v, xr, gr, gate_logits = jnp.split(proj, IN_OFFSETS, axis=-1)
    y_ssd, ssd_buf_new, ssd_h_new = ssd_mixer(
        z, xbc, dt_raw, ssd_buf, ssd_h, prm['ssd_conv_w'], prm['ssd_conv_b'],
        prm['ssd_dt_bias'], prm['ssd_a_log'], prm['ssd_d'], prm['ssd_norm_g'])
    q = q.reshape(Bsz, T, MOBA_N_HEADS, MOBA_HEAD_DIM)
    k = k.reshape(Bsz, T, MOBA_N_HEADS, MOBA_HEAD_DIM)
    v = v.reshape(Bsz, T, MOBA_N_HEADS, MOBA_HEAD_DIM)
    if past_k is None:
        k_all, v_all = k, v
    else:
        k_all = jnp.concatenate([past_k.astype(k.dtype), k], axis=1)
        v_all = jnp.concatenate([past_v.astype(v.dtype), v], axis=1)
    y_moba = moba_attend(q, k_all, v_all, q_pos)
    y_lru, lru_buf_new, lru_h_new = rglru_branch(
        xr, gr, lru_buf, lru_h, prm['lru_conv_w'], prm['lru_conv_b'], prm['lru_wa'],
        prm['lru_ba'], prm['lru_wx'], prm['lru_bx'], prm['lru_lambda'])
    gates = jax.nn.sigmoid(gate_logits.astype(F32)).reshape(Bsz, T, N_BRANCH, D_MODEL)
    merged = (gates[:, :, 0] * jnp.einsum('bte,ed->btd', y_ssd, prm['w_branch_ssd'])
              + gates[:, :, 1] * jnp.einsum('bte,ed->btd', y_moba, prm['w_branch_moba'])
              + gates[:, :, 2] * jnp.einsum('bte,ed->btd', y_lru, prm['w_branch_lru']))
    mix = jnp.einsum('btd,de->bte', merged.astype(x.dtype), prm['w_out'])
    x = x + gt1 * rmsnorm(mix, prm['g_post_mix'])
    h2 = rmsnorm(x, prm['g_pre_mlp']) * (1.0 + sc2) + sh2
    up = jax.nn.relu(jnp.einsum('btd,df->btf', h2, prm['w_up']))
    f = jnp.einsum('btf,fd->btd', up * up, prm['w_down'])
    x = x + gt2 * rmsnorm(f, prm['g_post_mlp'])
    return x, k, v, ssd_buf_new, ssd_h_new, lru_buf_new, lru_h_new


def setup_inputs(seed: int = 0) -> dict:
    key = jax.random.key(seed)
    keys = iter(jax.random.split(key, 64))

    def nrm(shape, scale=1.0):
        return scale * jax.random.normal(next(keys), shape, F32)

    def unif(shape, lo, hi):
        return jax.random.uniform(next(keys), shape, F32, lo, hi)

    n_pages = PAST_LEN // PAGE_SIZE
    n_pool = (DEC_BATCH * n_pages * 5) // 4
    perm = jax.random.permutation(next(keys), n_pool)
    page_table = perm[:DEC_BATCH * n_pages].reshape(DEC_BATCH, n_pages).astype(jnp.int32)
    dt0 = jnp.exp(unif((DEPTH, SSD_N_HEADS), math.log(1e-3), math.log(1e-1)))
    ssd_dt_bias = dt0 + jnp.log(-jnp.expm1(-dt0))
    a_c = unif((DEPTH, LRU_WIDTH), 0.9, 0.999)
    s = a_c ** (1.0 / LRU_C)
    lru_lambda = jnp.log(s) - jnp.log1p(-s)
    d = D_MODEL
    return {
        'x_prompt': nrm((BATCH, SEQ, d)),
        'x_sample': nrm((DEC_BATCH, DEC_SEQ, d)),
        'c_prompt': nrm((BATCH, d)),
        'c_sample': nrm((DEC_BATCH, d)),
        'cache_k': nrm((DEPTH, n_pool, PAGE_SIZE, MOBA_N_HEADS, MOBA_HEAD_DIM)),
        'cache_v': nrm((DEPTH, n_pool, PAGE_SIZE, MOBA_N_HEADS, MOBA_HEAD_DIM)),
        'page_table': page_table,
        'state_ssm': nrm((DEPTH, DEC_BATCH, SSD_N_HEADS, SSD_HEAD_DIM, SSD_D_STATE), 0.1),
        'state_ssm_conv': nrm((DEPTH, DEC_BATCH, SSD_CONV - 1, SSD_CONV_DIM)),
        'state_lru': nrm((DEPTH, DEC_BATCH, LRU_WIDTH), 0.5),
        'state_lru_conv': nrm((DEPTH, DEC_BATCH, LRU_CONV - 1, LRU_WIDTH)),
        'w_mod': nrm((DEPTH, d, 6 * d), d ** -0.5),
        'b_mod': nrm((DEPTH, 6 * d), 0.02),
        'g_pre_mix': 1.0 + nrm((DEPTH, d), 0.02),
        'g_post_mix': 1.0 + nrm((DEPTH, d), 0.02),
        'g_pre_mlp': 1.0 + nrm((DEPTH, d), 0.02),
        'g_post_mlp': 1.0 + nrm((DEPTH, d), 0.02),
        'w_in': nrm((DEPTH, d, N_IN), d ** -0.5),
        'ssd_conv_w': nrm((DEPTH, SSD_CONV, SSD_CONV_DIM), SSD_CONV ** -0.5),
        'ssd_conv_b': nrm((DEPTH, SSD_CONV_DIM), 0.02),
        'ssd_dt_bias': ssd_dt_bias,
        'ssd_a_log': jnp.log(unif((DEPTH, SSD_N_HEADS), 1.0, 16.0)),
        'ssd_d': 1.0 + nrm((DEPTH, SSD_N_HEADS), 0.02),
        'ssd_norm_g': 1.0 + nrm((DEPTH, SSD_D_INNER), 0.02),
        'lru_conv_w': nrm((DEPTH, LRU_CONV, LRU_WIDTH), LRU_CONV ** -0.5),
        'lru_conv_b': nrm((DEPTH, LRU_WIDTH), 0.02),
        'lru_wa': nrm((DEPTH, LRU_N_BLOCKS, LRU_BLOCK, LRU_BLOCK), LRU_BLOCK ** -0.5),
        'lru_ba': nrm((DEPTH, LRU_N_BLOCKS, LRU_BLOCK), 0.02),
        'lru_wx': nrm((DEPTH, LRU_N_BLOCKS, LRU_BLOCK, LRU_BLOCK), LRU_BLOCK ** -0.5),
        'lru_bx': nrm((DEPTH, LRU_N_BLOCKS, LRU_BLOCK), 0.02),
        'lru_lambda': lru_lambda,
        'w_branch_ssd': nrm((DEPTH, SSD_D_INNER, d), SSD_D_INNER ** -0.5),
        'w_branch_moba': nrm((DEPTH, MOBA_WIDTH, d), MOBA_WIDTH ** -0.5),
        'w_branch_lru': nrm((DEPTH, LRU_WIDTH, d), LRU_WIDTH ** -0.5),
        'w_out': nrm((DEPTH, d, d), d ** -0.5),
        'w_up': nrm((DEPTH, d, MLP_HIDDEN), d ** -0.5),
        'w_down': nrm((DEPTH, MLP_HIDDEN, d), MLP_HIDDEN ** -0.5),
    }


def reference(x_prompt, x_sample, c_prompt, c_sample, cache_k, cache_v, page_table,
              state_ssm, state_ssm_conv, state_lru, state_lru_conv,
              w_mod, b_mod, g_pre_mix, g_post_mix, g_pre_mlp, g_post_mlp, w_in,
              ssd_conv_w, ssd_conv_b, ssd_dt_bias, ssd_a_log, ssd_d, ssd_norm_g,
              lru_conv_w, lru_conv_b, lru_wa, lru_ba, lru_wx, lru_bx, lru_lambda,
              w_branch_ssd, w_branch_moba, w_branch_lru, w_out, w_up, w_down):
    Bp, Tp, _ = x_prompt.shape
    Bs, Ts, _ = x_sample.shape
    past_len = page_table.shape[1] * PAGE_SIZE
    pos_p = jnp.arange(Tp, dtype=jnp.int32)
    pos_s = past_len + jnp.arange(Ts, dtype=jnp.int32)
    dt = x_prompt.dtype
    zero_ssd_buf = jnp.zeros((Bp, SSD_CONV - 1, SSD_CONV_DIM), dt)
    zero_ssd_h = jnp.zeros((Bp, SSD_N_HEADS, SSD_HEAD_DIM, SSD_D_STATE), F32)
    zero_lru_buf = jnp.zeros((Bp, LRU_CONV - 1, LRU_WIDTH), dt)
    zero_lru_h = jnp.zeros((Bp, LRU_WIDTH), F32)

    kp, vp, ks, vs = [], [], [], []
    ssm_p, ssmc_p, lru_p, lruc_p = [], [], [], []
    ssm_s, ssmc_s, lru_s, lruc_s = [], [], [], []
    xp, xs = x_prompt, x_sample
    for l in range(DEPTH):
        prm = {
            'w_mod': w_mod[l], 'b_mod': b_mod[l],
            'g_pre_mix': g_pre_mix[l], 'g_post_mix': g_post_mix[l],
            'g_pre_mlp': g_pre_mlp[l], 'g_post_mlp': g_post_mlp[l], 'w_in': w_in[l],
            'ssd_conv_w': ssd_conv_w[l], 'ssd_conv_b': ssd_conv_b[l], 'ssd_dt_bias': ssd_dt_bias[l],
            'ssd_a_log': ssd_a_log[l], 'ssd_d': ssd_d[l], 'ssd_norm_g': ssd_norm_g[l],
            'lru_conv_w': lru_conv_w[l], 'lru_conv_b': lru_conv_b[l], 'lru_wa': lru_wa[l],
            'lru_ba': lru_ba[l], 'lru_wx': lru_wx[l], 'lru_bx': lru_bx[l], 'lru_lambda': lru_lambda[l],
            'w_branch_ssd': w_branch_ssd[l], 'w_branch_moba': w_branch_moba[l],
            'w_branch_lru': w_branch_lru[l], 'w_out': w_out[l], 'w_up': w_up[l], 'w_down': w_down[l],
        }
        xp, k_new, v_new, sb, sh, lb, lh = decoder_layer(
            xp, c_prompt, None, None, pos_p, zero_ssd_buf, zero_ssd_h, zero_lru_buf, zero_lru_h, prm)
        kp.append(k_new); vp.append(v_new)
        ssmc_p.append(sb); ssm_p.append(sh); lruc_p.append(lb); lru_p.append(lh)
        past_k = cache_k[l, page_table].reshape(Bs, past_len, MOBA_N_HEADS, MOBA_HEAD_DIM)
        past_v = cache_v[l, page_table].reshape(Bs, past_len, MOBA_N_HEADS, MOBA_HEAD_DIM)
        xs, k_new, v_new, sb, sh, lb, lh = decoder_layer(
            xs, c_sample, past_k, past_v, pos_s, state_ssm_conv[l], state_ssm[l],
            state_lru_conv[l], state_lru[l], prm)
        ks.append(k_new); vs.append(v_new)
        ssmc_s.append(sb); ssm_s.append(sh); lruc_s.append(lb); lru_s.append(lh)

    return (xp, xs,
            jnp.stack(kp), jnp.stack(vp), jnp.stack(ks), jnp.stack(vs),
            jnp.stack(ssm_p), jnp.stack(ssmc_p), jnp.stack(lru_p), jnp.stack(lruc_p),
            jnp.stack(ssm_s), jnp.stack(ssmc_s), jnp.stack(lru_s), jnp.stack(lruc_s))
```

```python
import functools
import math

import jax
import jax.numpy as jnp
from jax import lax
from jax.experimental import pallas as pl
from jax.experimental.pallas import tpu as pltpu

F32 = jnp.float32
BF16 = jnp.bfloat16
HIGHEST = lax.Precision.HIGHEST

D_MODEL = 1024
PAGE_SIZE = 128
SSD_D_INNER = D_MODEL
SSD_HEAD_DIM = 64
SSD_N_HEADS = SSD_D_INNER // SSD_HEAD_DIM
SSD_N_GROUPS = 4
SSD_D_STATE = 128
SSD_CONV = 4
SSD_CHUNK = 128
SSD_CONV_DIM = SSD_D_INNER + 2 * SSD_N_GROUPS * SSD_D_STATE
MOBA_N_HEADS = 8
MOBA_HEAD_DIM = 128
MOBA_WIDTH = MOBA_N_HEADS * MOBA_HEAD_DIM
MOBA_BLOCK = 256
MOBA_TOPK = 3
LRU_WIDTH = D_MODEL
LRU_N_BLOCKS = 16
LRU_BLOCK = LRU_WIDTH // LRU_N_BLOCKS
LRU_CONV = 4
LRU_C = 8.0
N_BRANCH = 3
MLP_HIDDEN = 4 * D_MODEL
RMS_EPS = 1e-6

SUBLANES = 8
LANES = 128
VMEM_LIMIT_BYTES = 56 * 1024 * 1024

SAMPLE_T_PAD = SUBLANES
CONV_HALO = SUBLANES
NEG_BIG = -1e30

COL_XBC = 0
COL_Z = COL_XBC + SSD_CONV_DIM
COL_Q = COL_Z + SSD_D_INNER
COL_K = COL_Q + MOBA_WIDTH
COL_V = COL_K + MOBA_WIDTH
COL_XR = COL_V + MOBA_WIDTH
COL_GR = COL_XR + LRU_WIDTH
COL_GATE = COL_GR + LRU_WIDTH
P_WIDTH = COL_GATE + N_BRANCH * D_MODEL
PROJ_TN = 1024
DT_PAD = LANES


def _cparams(sem):
    return pltpu.CompilerParams(dimension_semantics=sem, vmem_limit_bytes=VMEM_LIMIT_BYTES)


def _sds(shape, dtype=F32):
    return jax.ShapeDtypeStruct(shape, dtype)


def _silu(x):
    return x * jax.nn.sigmoid(x)


def _softplus(x):
    return jnp.maximum(x, 0.0) + jnp.log1p(jnp.exp(-jnp.abs(x)))


def _rms(x):
    return x * lax.rsqrt(jnp.mean(x * x, axis=-1, keepdims=True) + RMS_EPS)


def _dot(a, b):
    return jnp.dot(a, b, preferred_element_type=F32)


def _dot_nt(a, b, precision=None):
    return lax.dot_general(a, b, (((1,), (1,)), ((), ())), precision=precision,
                           preferred_element_type=F32)


def _mod_kernel(c_ref, w_ref, b_ref, o_ref):
    s = _silu(c_ref[...]).astype(BF16)
    o_ref[0] = _dot(s, w_ref[0].astype(BF16)) + b_ref[0]


def _mod_call(c_all, w_mod, b_mod):
    depth, d, n = w_mod.shape
    rows = c_all.shape[0]
    tn = 1536
    return pl.pallas_call(
        _mod_kernel,
        grid=(depth, n // tn),
        in_specs=[pl.BlockSpec((rows, d), lambda l, j: (0, 0)),
                  pl.BlockSpec((1, d, tn), lambda l, j: (l, 0, j)),
                  pl.BlockSpec((1, 1, tn), lambda l, j: (l, 0, j))],
        out_specs=pl.BlockSpec((1, rows, tn), lambda l, j: (l, 0, j)),
        out_shape=_sds((depth, rows, n)),
        compiler_params=_cparams(("arbitrary", "arbitrary")),
        name="mod",
    )(c_all, w_mod, b_mod.reshape(depth, 1, n))


def _inproj_kernel(x_ref, sc_ref, sh_ref, g_ref, w_ref, wdt_ref, p_ref, dt_ref, h_scr):
    @pl.when(pl.program_id(1) == 0)
    def _():
        y = _rms(x_ref[...]) * g_ref[...]
        hb = (y * (1.0 + sc_ref[0]) + sh_ref[0]).astype(BF16)
        h_scr[...] = hb
        dt_ref[...] = _dot(hb, wdt_ref[...])

    p_ref[...] = _dot(h_scr[...], w_ref[...])


def _inproj_call(x2, sc, sh, g, w_all, wdt, *, tm, tiles_per_mod):
    rows, d = x2.shape
    mod_rows = sc.shape[1]
    mod_spec = pl.BlockSpec((1, mod_rows, d), lambda i, j: (i // tiles_per_mod, 0, 0))
    return pl.pallas_call(
        _inproj_kernel,
        grid=(rows // tm, P_WIDTH // PROJ_TN),
        in_specs=[pl.BlockSpec((tm, d), lambda i, j: (i, 0)),
                  mod_spec, mod_spec,
                  pl.BlockSpec((1, d), lambda i, j: (0, 0)),
                  pl.BlockSpec((d, PROJ_TN), lambda i, j: (0, j)),
                  pl.BlockSpec((d, DT_PAD), lambda i, j: (0, 0))],
        out_specs=[pl.BlockSpec((tm, PROJ_TN), lambda i, j: (i, j)),
                   pl.BlockSpec((tm, DT_PAD), lambda i, j: (i, 0))],
        out_shape=[_sds((rows, P_WIDTH)), _sds((rows, DT_PAD))],
        scratch_shapes=[pltpu.VMEM((tm, d), BF16)],
        compiler_params=_cparams(("arbitrary", "arbitrary")),
        name="inproj",
    )(x2, sc, sh, g, w_all, wdt)


def _conv_tile(xp_scr, x_rows, buf_ref, w_ref, b_ref, tile_idx, rows_in, rows):
    @pl.when(tile_idx == 0)
    def _():
        xp_scr[0:CONV_HALO, :] = buf_ref[0]

    @pl.when(tile_idx > 0)
    def _():
        xp_scr[0:CONV_HALO, :] = xp_scr[rows:rows + CONV_HALO, :]

    xp_scr[CONV_HALO:CONV_HALO + rows_in, :] = x_rows
    if rows_in < rows:
        xp_scr[CONV_HALO + rows_in:CONV_HALO + rows, :] = jnp.zeros(
            (rows - rows_in, xp_scr.shape[1]), F32)
    first = CONV_HALO - (SSD_CONV - 1)
    acc = b_ref[...] + w_ref[0:1, :] * xp_scr[first:first + rows, :]
    for tap in range(1, SSD_CONV):
        acc = acc + w_ref[tap:tap + 1, :] * xp_scr[first + tap:first + tap + rows, :]
    return acc


def _ssd_kernel(xbc_ref, z_ref, dt_ref, buf_ref, h0_ref, cw_ref, cb_ref, dtb_ref, alog_ref,
                dsk_ref, ng_ref, y_ref, h_ref, xp_scr, dt_scr, *, rows_in, t_valid):
    L = SSD_CHUNK
    N = SSD_D_STATE
    P2 = 2 * SSD_HEAD_DIM
    c = pl.program_id(1)

    @pl.when(c == 0)
    def _():
        h_ref[0] = h0_ref[0]

    xc = _silu(_conv_tile(xp_scr, xbc_ref[0], buf_ref, cw_ref, cb_ref, c, rows_in, L))
    xs = xc[:, :SSD_D_INNER]
    bm = xc[:, SSD_D_INNER:SSD_D_INNER + SSD_N_GROUPS * N]
    cm = xc[:, SSD_D_INNER + SSD_N_GROUPS * N:]

    if rows_in < L:
        dt_scr[...] = jnp.zeros((L, LANES), F32)
    dt_scr[0:rows_in, :] = dt_ref[0]
    lane = lax.broadcasted_iota(jnp.int32, (L, LANES), 1)
    row = lax.broadcasted_iota(jnp.int32, (L, LANES), 0)
    dtv = _softplus(dt_scr[...] + dtb_ref[...])
    a = jnp.where(lane < SSD_N_HEADS, -jnp.exp(alog_ref[...]), 0.0)
    dtv = jnp.where((row + c * L < t_valid) & (lane < SSD_N_HEADS), dtv, 0.0)
    dta = dtv * a
    ri = lax.broadcasted_iota(jnp.int32, (L, L), 0)
    ci = lax.broadcasted_iota(jnp.int32, (L, L), 1)
    causal = ri >= ci
    cum = jnp.dot(causal.astype(F32), dta, precision=HIGHEST, preferred_element_type=F32)
    cum_t = cum.T
    dtv_t = dtv.T
    ecum = jnp.exp(cum)
    cum_last = cum[L - 1:L, :]
    wgt = jnp.exp(cum_last - cum) * dtv
    cdec_t = jnp.exp(cum_t[:, L - 1:L])
    left = lax.broadcasted_iota(jnp.int32, (L, P2), 1) < SSD_HEAD_DIM
    left_rows = lax.broadcasted_iota(jnp.int32, (P2, N), 0) < SSD_HEAD_DIM

    y_tiles = []
    for j in range(SSD_N_HEADS // 2):
        g = j // (SSD_N_HEADS // SSD_N_GROUPS // 2)
        h_a, h_b = 2 * j, 2 * j + 1
        bm_g = bm[:, g * N:(g + 1) * N]
        cm_g = cm[:, g * N:(g + 1) * N].astype(BF16)
        cb = _dot_nt(cm_g, bm_g.astype(BF16))
        xs_p = xs[:, j * P2:(j + 1) * P2]
        m_parts = []
        for hh in (h_a, h_b):
            seg = cum[:, hh:hh + 1] - cum_t[hh:hh + 1, :]
            dec = jnp.exp(jnp.where(causal, seg, -jnp.inf))
            m_parts.append((cb * dec * dtv_t[hh:hh + 1, :]).astype(BF16))
        m_pair = jnp.concatenate(m_parts, axis=1)
        x_bd = jnp.concatenate([jnp.where(left, xs_p, 0.0), jnp.where(left, 0.0, xs_p)],
                               axis=0).astype(BF16)
        y_diag = _dot(m_pair, x_bd)
        h_prev = h_ref[0, j * P2:(j + 1) * P2, :]
        y_off = _dot_nt(cm_g, h_prev.astype(BF16))
        e_sel = jnp.where(left, ecum[:, h_a:h_a + 1], ecum[:, h_b:h_b + 1])
        y_tiles.append(y_diag + y_off * e_sel + dsk_ref[:, j * P2:(j + 1) * P2] * xs_p)
        w_sel = jnp.where(left, wgt[:, h_a:h_a + 1], wgt[:, h_b:h_b + 1])
        xw_t = (xs_p * w_sel).T.astype(BF16)
        st = _dot(xw_t, bm_g.astype(BF16))
        d_sel = jnp.where(left_rows, cdec_t[h_a:h_a + 1, :], cdec_t[h_b:h_b + 1, :])
        h_ref[0, j * P2:(j + 1) * P2, :] = h_prev * d_sel + st

    gw = SSD_D_INNER // SSD_N_GROUPS
    tiles_per_group = gw // P2
    zs = _silu(z_ref[0])
    for g in range(SSD_N_GROUPS):
        yg = jnp.concatenate(y_tiles[g * tiles_per_group:(g + 1) * tiles_per_group], axis=1)
        yg = yg[0:rows_in, :] * zs[:, g * gw:(g + 1) * gw]
        y_ref[0, :, g * gw:(g + 1) * gw] = _rms(yg) * ng_ref[:, g * gw:(g + 1) * gw]


def _ssd_call(p3, dt3, buf8, h0, cw, cb, dtb, alog, dsk, ng, *, rows_in, t_valid):
    bsz, t_pad, _ = p3.shape
    nc = t_pad // rows_in
    L = SSD_CHUNK
    hp = SSD_N_HEADS * SSD_HEAD_DIM
    const2 = lambda b, c: (0, 0)
    kern = functools.partial(_ssd_kernel, rows_in=rows_in, t_valid=t_valid)
    return pl.pallas_call(
        kern,
        grid=(bsz, nc),
        in_specs=[pl.BlockSpec((1, rows_in, SSD_CONV_DIM), lambda b, c: (b, c, COL_XBC // SSD_CONV_DIM)),
                  pl.BlockSpec((1, rows_in, SSD_D_INNER), lambda b, c: (b, c, COL_Z // SSD_D_INNER)),
                  pl.BlockSpec((1, rows_in, DT_PAD), lambda b, c: (b, c, 0)),
                  pl.BlockSpec((1, CONV_HALO, SSD_CONV_DIM), lambda b, c: (b, 0, 0)),
                  pl.BlockSpec((1, hp, SSD_D_STATE), lambda b, c: (b, 0, 0)),
                  pl.BlockSpec((SSD_CONV, SSD_CONV_DIM), const2),
                  pl.BlockSpec((1, SSD_CONV_DIM), const2),
                  pl.BlockSpec((1, DT_PAD), const2),
                  pl.BlockSpec((1, DT_PAD), const2),
                  pl.BlockSpec((1, SSD_D_INNER), const2),
                  pl.BlockSpec((1, SSD_D_INNER), const2)],
        out_specs=[pl.BlockSpec((1, rows_in, SSD_D_INNER), lambda b, c: (b, c, 0)),
                   pl.BlockSpec((1, hp, SSD_D_STATE), lambda b, c: (b, 0, 0))],
        out_shape=[_sds((bsz, t_pad, SSD_D_INNER)), _sds((bsz, hp, SSD_D_STATE))],
        scratch_shapes=[pltpu.VMEM((CONV_HALO + L, SSD_CONV_DIM), F32),
                        pltpu.VMEM((L, DT_PAD), F32)],
        compiler_params=_cparams(("arbitrary", "arbitrary")),
        name="ssd",
    )(p3, p3, dt3, buf8, h0, cw, cb, dtb, alog, dsk, ng)


def _lru_kernel(xr_ref, gr_ref, buf_ref, h0_ref, cw_ref, cb_ref, wa_ref, wx_ref, ba_ref, bx_ref,
                lam_ref, y_ref, h_ref, xp_scr, *, tm, t_valid):
    i = pl.program_id(1)

    @pl.when(i == 0)
    def _():
        h_ref[0] = h0_ref[0]

    xc = _conv_tile(xp_scr, xr_ref[0], buf_ref, cw_ref, cb_ref, i, tm, tm)
    bw = wa_ref.shape[1]
    r_parts, i_parts = [], []
    for j in range(LRU_WIDTH // bw):
        xj = xc[:, j * bw:(j + 1) * bw].astype(BF16)
        r_parts.append(_dot(xj, wa_ref[j]))
        i_parts.append(_dot(xj, wx_ref[j]))
    r = jax.nn.sigmoid(jnp.concatenate(r_parts, axis=1) + ba_ref[...])
    ig = jax.nn.sigmoid(jnp.concatenate(i_parts, axis=1) + bx_ref[...])
    log_a = (-LRU_C) * r * _softplus(-lam_ref[...])
    a_full = jnp.exp(log_a)
    b_full = jnp.sqrt(-jnp.tanh(log_a) * (a_full * a_full + 1.0)) * (ig * xc)
    gate = jax.nn.gelu(gr_ref[0], approximate=True)

    row = lax.broadcasted_iota(jnp.int32, (tm, LANES), 0)
    valid = row + i * tm < t_valid
    for s in range(LRU_WIDTH // LANES):
        sl = slice(s * LANES, (s + 1) * LANES)
        a = jnp.where(valid, a_full[:, sl], 1.0)
        b = jnp.where(valid, b_full[:, sl], 0.0)
        k = 1
        while k < tm:
            keep = row >= k
            b = b + a * jnp.where(keep, pltpu.roll(b, k, 0), 0.0)
            a = a * jnp.where(keep, pltpu.roll(a, k, 0), 1.0)
            k *= 2
        h = b + a * h_ref[0, :, sl]
        y_ref[0, :, sl] = h * gate[:, sl]
        h_ref[0, :, sl] = h[tm - 1:tm, :]


def _lru_call(p3, buf8, h0, cw, cb, wa_bd, wx_bd, ba, bx, lam, *, tm, t_valid):
    bsz, t_pad, _ = p3.shape
    W = LRU_WIDTH
    nb, bw, _ = wa_bd.shape
    const2 = lambda b, i: (0, 0)
    const3 = lambda b, i: (0, 0, 0)
    kern = functools.partial(_lru_kernel, tm=tm, t_valid=t_valid)
    return pl.pallas_call(
        kern,
        grid=(bsz, t_pad // tm),
        in_specs=[pl.BlockSpec((1, tm, W), lambda b, i: (b, i, COL_XR // W)),
                  pl.BlockSpec((1, tm, W), lambda b, i: (b, i, COL_GR // W)),
                  pl.BlockSpec((1, CONV_HALO, W), lambda b, i: (b, 0, 0)),
                  pl.BlockSpec((1, 1, W), lambda b, i: (b, 0, 0)),
                  pl.BlockSpec((LRU_CONV, W), const2),
                  pl.BlockSpec((1, W), const2),
                  pl.BlockSpec((nb, bw, bw), const3),
                  pl.BlockSpec((nb, bw, bw), const3),
                  pl.BlockSpec((1, W), const2),
                  pl.BlockSpec((1, W), const2),
                  pl.BlockSpec((1, W), const2)],
        out_specs=[pl.BlockSpec((1, tm, W), lambda b, i: (b, i, 0)),
                   pl.BlockSpec((1, 1, W), lambda b, i: (b, 0, 0))],
        out_shape=[_sds((bsz, t_pad, W)), _sds((bsz, 1, W))],
        scratch_shapes=[pltpu.VMEM((CONV_HALO + tm, W), F32)],
        compiler_params=_cparams(("arbitrary", "arbitrary")),
        name="lru",
    )(p3, p3, buf8, h0, cw, cb, wa_bd, wx_bd, ba, bx, lam)


def _topk_mask(gate, n_valid, col):
    nb = gate.shape[1]
    past = col < n_valid
    gm = jnp.where(past, gate, -jnp.inf)
    rank = jnp.zeros(gate.shape, F32)
    for j in range(nb):
        gj = gm[:, j:j + 1]
        ahead = (gj > gm) | ((gj == gm) & (col > j))
        rank = rank + jnp.where(ahead, 1.0, 0.0)
    return (rank < MOBA_TOPK) & past, rank


def _moba_prompt_kernel(q_ref, k_ref, v_ref, o_ref, kmean_scr, *, nb):
    i = pl.program_id(2)
    BS = MOBA_BLOCK
    scale = MOBA_HEAD_DIM ** -0.5

    @pl.when(i == 0)
    def _():
        for j in range(nb):
            kmean_scr[j:j + 1, :] = jnp.mean(k_ref[0, j * BS:(j + 1) * BS, :], axis=0, keepdims=True)

    q = q_ref[0]
    qb = q.astype(BF16)
    gate = _dot_nt(q, kmean_scr[...], precision=HIGHEST)
    col = lax.broadcasted_iota(jnp.int32, gate.shape, 1)
    sel, _ = _topk_mask(gate, i, col)
    self32 = jnp.where(sel, 1.0, 0.0)

    row0 = pl.multiple_of(i * BS, BS)
    kb = k_ref[0, pl.ds(row0, BS), :].astype(BF16)
    vb = v_ref[0, pl.ds(row0, BS), :].astype(BF16)
    s = _dot_nt(qb, kb) * scale
    ri = lax.broadcasted_iota(jnp.int32, (BS, BS), 0)
    ci = lax.broadcasted_iota(jnp.int32, (BS, BS), 1)
    s = jnp.where(ri >= ci, s, NEG_BIG)
    m0 = jnp.max(s, axis=1, keepdims=True)
    p = jnp.exp(s - m0)
    l0 = jnp.sum(p, axis=1, keepdims=True)
    acc0 = _dot(p.astype(BF16), vb)

    def body(j, carry):
        m, l, acc = carry
        r0 = pl.multiple_of(j * BS, BS)
        kj = k_ref[0, pl.ds(r0, BS), :].astype(BF16)
        vj = v_ref[0, pl.ds(r0, BS), :].astype(BF16)
        sj = _dot_nt(qb, kj) * scale
        on = jnp.max(jnp.where(col == j, self32, 0.0), axis=1, keepdims=True) > 0.0
        sj = jnp.where(on, sj, NEG_BIG)
        m_new = jnp.maximum(m, jnp.max(sj, axis=1, keepdims=True))
        alpha = jnp.exp(m - m_new)
        pj = jnp.exp(sj - m_new)
        l_new = alpha * l + jnp.sum(pj, axis=1, keepdims=True)
        acc_new = alpha * acc + _dot(pj.astype(BF16), vj)
        return m_new, l_new, acc_new

    m, l, acc = lax.fori_loop(0, i, body, (m0, l0, acc0))
    o_ref[0] = acc / l


def _moba_prompt_call(p3):
    bsz, t, _ = p3.shape
    assert t % MOBA_BLOCK == 0
    nb = t // MOBA_BLOCK
    dh = MOBA_HEAD_DIM
    kern = functools.partial(_moba_prompt_kernel, nb=nb)
    return pl.pallas_call(
        kern,
        grid=(bsz, MOBA_N_HEADS, nb),
        in_specs=[pl.BlockSpec((1, MOBA_BLOCK, dh), lambda b, h, i: (b, i, COL_Q // dh + h)),
                  pl.BlockSpec((1, t, dh), lambda b, h, i: (b, 0, COL_K // dh + h)),
                  pl.BlockSpec((1, t, dh), lambda b, h, i: (b, 0, COL_V // dh + h))],
        out_specs=pl.BlockSpec((1, MOBA_BLOCK, dh), lambda b, h, i: (b, i, h)),
        out_shape=_sds((bsz, t, MOBA_WIDTH)),
        scratch_shapes=[pltpu.VMEM((nb, dh), F32)],
        compiler_params=_cparams(("arbitrary", "arbitrary", "arbitrary")),
        name="moba_prompt",
    )(p3, p3, p3)


PAGES_PER_BLOCK = MOBA_BLOCK // PAGE_SIZE
BLOCKS_PER_STEP = SUBLANES
PAGES_PER_STEP = BLOCKS_PER_STEP * PAGES_PER_BLOCK


def _ksum_kernel(pt_ref, *refs):
    page_refs, o_ref = refs[:PAGES_PER_STEP], refs[PAGES_PER_STEP]
    for j in range(BLOCKS_PER_STEP):
        acc = jnp.sum(page_refs[j * PAGES_PER_BLOCK][0, 0], axis=0)
        for u in range(1, PAGES_PER_BLOCK):
            acc = acc + jnp.sum(page_refs[j * PAGES_PER_BLOCK + u][0, 0], axis=0)
        for h in range(MOBA_N_HEADS):
            o_ref[0, h, j:j + 1, :] = acc[h:h + 1, :]


def _ksum_call(page_table, cache, layer):
    bs, n_pages = page_table.shape
    assert n_pages % PAGES_PER_STEP == 0

    def page_spec(p):
        return pl.BlockSpec((1, 1, PAGE_SIZE, MOBA_N_HEADS, MOBA_HEAD_DIM),
                            lambda b, g, pt: (layer, pt[b, g * PAGES_PER_STEP + p], 0, 0, 0))

    grid_spec = pltpu.PrefetchScalarGridSpec(
        num_scalar_prefetch=1,
        grid=(bs, n_pages // PAGES_PER_STEP),
        in_specs=[page_spec(p) for p in range(PAGES_PER_STEP)],
        out_specs=pl.BlockSpec((1, MOBA_N_HEADS, BLOCKS_PER_STEP, MOBA_HEAD_DIM),
                               lambda b, g, pt: (b, 0, g, 0)),
    )
    return pl.pallas_call(
        _ksum_kernel,
        grid_spec=grid_spec,
        out_shape=_sds((bs, MOBA_N_HEADS, n_pages // PAGES_PER_BLOCK, MOBA_HEAD_DIM)),
        compiler_params=_cparams(("arbitrary", "arbitrary")),
        name="moba_ksum",
    )(page_table, *([cache] * PAGES_PER_STEP))


def _select_kernel(q_ref, ks_ref, o_ref, *, nb):
    rows = q_ref.shape[1]
    col = lax.broadcasted_iota(jnp.int32, (rows, nb), 1)
    colf = col.astype(F32)
    lane = lax.broadcasted_iota(jnp.int32, (rows, LANES), 1)
    out = jnp.zeros((rows, LANES), F32)
    for h in range(MOBA_N_HEADS):
        sl = slice(h * MOBA_HEAD_DIM, (h + 1) * MOBA_HEAD_DIM)
        kmean = ks_ref[0, h] * (1.0 / MOBA_BLOCK)
        gate = _dot_nt(q_ref[0, :, sl], kmean, precision=HIGHEST)
        sel, rank = _topk_mask(gate, nb, col)
        for r in range(MOBA_TOPK):
            idx = jnp.sum(jnp.where(sel & (rank == r), colf, 0.0), axis=1, keepdims=True)
            out = jnp.where(lane == h * 4 + r, idx, out)
    o_ref[0] = out.astype(jnp.int32)


def _select_call(p3, ksum):
    bs, rows, _ = p3.shape
    nb = ksum.shape[2]
    assert nb >= MOBA_TOPK
    return pl.pallas_call(
        functools.partial(_select_kernel, nb=nb),
        grid=(bs,),
        in_specs=[pl.BlockSpec((1, rows, MOBA_WIDTH), lambda b: (b, 0, COL_Q // MOBA_WIDTH)),
                  pl.BlockSpec((1, MOBA_N_HEADS, nb, MOBA_HEAD_DIM), lambda b: (b, 0, 0, 0))],
        out_specs=pl.BlockSpec((1, rows, LANES), lambda b: (b, 0, 0)),
        out_shape=_sds((bs, rows, LANES), jnp.int32),
        compiler_params=_cparams(("arbitrary",)),
        name="moba_select",
    )(p3, ksum)


N_SEL_PAGES = MOBA_TOPK * PAGES_PER_BLOCK


def _moba_sample_kernel(pt_ref, sel_ref, q_ref, kn_ref, vn_ref, ck_hbm, cv_hbm, o_ref,
                        kbuf, vbuf, sem, *, layer, t_valid):
    step = pl.program_id(0)
    n_steps = pl.num_programs(0)
    t = step % t_valid
    slot = step % 2
    rows = q_ref.shape[1]
    dh = MOBA_HEAD_DIM
    scale = MOBA_HEAD_DIM ** -0.5

    def gather(st, sl, start):
        b = st // t_valid
        for h in range(MOBA_N_HEADS):
            for r in range(MOBA_TOPK):
                blk = sel_ref[(st * MOBA_N_HEADS + h) * MOBA_TOPK + r] if start else 0
                for half in range(PAGES_PER_BLOCK):
                    page = pt_ref[b, blk * PAGES_PER_BLOCK + half] if start else 0
                    n = r * PAGES_PER_BLOCK + half
                    for src, dst in ((ck_hbm, kbuf), (cv_hbm, vbuf)):
                        cp = pltpu.make_async_copy(src.at[layer, page, :, h, :], dst.at[sl, h, n],
                                                   sem.at[sl])
                        if start:
                            cp.start()
                        else:
                            cp.wait()

    @pl.when(step == 0)
    def _():
        gather(step, slot, True)

    @pl.when(step + 1 < n_steps)
    def _():
        gather(step + 1, 1 - slot, True)

    gather(step, slot, False)

    @pl.when(t == 0)
    def _():
        o_ref[...] = jnp.zeros(o_ref.shape, F32)

    u = lax.broadcasted_iota(jnp.int32, (rows, dh), 0)
    own_ok = (u <= t) & (u < t_valid)
    is_t = u == t
    for h in range(MOBA_N_HEADS):
        sl = slice(h * dh, (h + 1) * dh)
        q_row = jnp.sum(jnp.where(is_t, q_ref[0, :, sl], 0.0), axis=0, keepdims=True)
        q_rep = jnp.broadcast_to(q_row, (dh, dh)).astype(BF16)
        s_own = jnp.where(own_ok, _dot_nt(kn_ref[0, :, sl].astype(BF16), q_rep) * scale, NEG_BIG)
        s_parts = [_dot_nt(kbuf[slot, h, n].astype(BF16), q_rep) * scale for n in range(N_SEL_PAGES)]
        m = jnp.max(s_own, axis=0, keepdims=True)
        for sp in s_parts:
            m = jnp.maximum(m, jnp.max(sp, axis=0, keepdims=True))
        p_own = jnp.exp(s_own - m)
        l = jnp.sum(p_own, axis=0, keepdims=True)
        acc = jnp.sum(p_own * vn_ref[0, :, sl], axis=0, keepdims=True)
        for n, sp in enumerate(s_parts):
            pp = jnp.exp(sp - m)
            l = l + jnp.sum(pp, axis=0, keepdims=True)
            acc = acc + jnp.sum(pp * vbuf[slot, h, n], axis=0, keepdims=True)
        o_ref[0, :, sl] = jnp.where(is_t, acc / l, o_ref[0, :, sl])


def _moba_sample_call(page_table, sel, p3, cache_k, cache_v, layer, *, t_valid):
    bs, rows, _ = p3.shape
    W = MOBA_WIDTH

    def new_spec(col):
        return pl.BlockSpec((1, rows, W), lambda s, pt, sl: (s // t_valid, 0, col // W))

    buf_shape = (2, MOBA_N_HEADS, N_SEL_PAGES, PAGE_SIZE, MOBA_HEAD_DIM)
    grid_spec = pltpu.PrefetchScalarGridSpec(
        num_scalar_prefetch=2,
        grid=(bs * t_valid,),
        in_specs=[new_spec(COL_Q), new_spec(COL_K), new_spec(COL_V),
                  pl.BlockSpec(memory_space=pl.ANY), pl.BlockSpec(memory_space=pl.ANY)],
        out_specs=pl.BlockSpec((1, rows, W), lambda s, pt, sl: (s // t_valid, 0, 0)),
        scratch_shapes=[pltpu.VMEM(buf_shape, F32), pltpu.VMEM(buf_shape, F32),
                        pltpu.SemaphoreType.DMA((2,))],
    )
    return pl.pallas_call(
        functools.partial(_moba_sample_kernel, layer=layer, t_valid=t_valid),
        grid_spec=grid_spec,
        out_shape=_sds((bs, rows, W)),
        compiler_params=_cparams(("arbitrary",)),
        name="moba_sample",
    )(page_table, sel.reshape(-1), p3, p3, p3, cache_k, cache_v)


def _merge_kernel(ys_ref, ym_ref, yl_ref, g0_ref, g1_ref, g2_ref, x_ref, gt_ref, gp_ref,
                  ws_ref, wm_ref, wl_ref, wo_ref, o_ref):
    merged = (jax.nn.sigmoid(g0_ref[...]) * _dot(ys_ref[...].astype(BF16), ws_ref[...])
              + jax.nn.sigmoid(g1_ref[...]) * _dot(ym_ref[...].astype(BF16), wm_ref[...])
              + jax.nn.sigmoid(g2_ref[...]) * _dot(yl_ref[...].astype(BF16), wl_ref[...]))
    mix = _dot(merged.astype(BF16), wo_ref[...])
    o_ref[...] = x_ref[...] + gt_ref[0] * (_rms(mix) * gp_ref[...])


def _merge_call(ys, ym, yl, p2, x2, gt, gp, ws, wm, wl, wo, *, tm, tiles_per_mod):
    rows, d = x2.shape
    mod_rows = gt.shape[1]
    row_spec = pl.BlockSpec((tm, d), lambda i: (i, 0))
    w_spec = pl.BlockSpec((d, d), lambda i: (0, 0))

    def gate_spec(n):
        return pl.BlockSpec((tm, d), lambda i: (i, COL_GATE // d + n))

    return pl.pallas_call(
        _merge_kernel,
        grid=(rows // tm,),
        in_specs=[row_spec, row_spec, row_spec, gate_spec(0), gate_spec(1), gate_spec(2), row_spec,
                  pl.BlockSpec((1, mod_rows, d), lambda i: (i // tiles_per_mod, 0, 0)),
                  pl.BlockSpec((1, d), lambda i: (0, 0)),
                  w_spec, w_spec, w_spec, w_spec],
        out_specs=row_spec,
        out_shape=_sds((rows, d)),
        compiler_params=_cparams(("arbitrary",)),
        name="merge",
    )(ys, ym, yl, p2, p2, p2, x2, gt, gp, ws, wm, wl, wo)


def _mlp_kernel(x_ref, sc_ref, sh_ref, gt_ref, g1_ref, g2_ref, wu_ref, wd_ref, o_ref):
    x = x_ref[...]
    h2 = (_rms(x) * g1_ref[...]) * (1.0 + sc_ref[0]) + sh_ref[0]
    up = jnp.maximum(_dot(h2.astype(BF16), wu_ref[...]), 0.0)
    f = _dot((up * up).astype(BF16), wd_ref[...])
    o_ref[...] = x + gt_ref[0] * (_rms(f) * g2_ref[...])


def _mlp_call(x2, sc, sh, gt, g1, g2, wu, wd, *, tm, tiles_per_mod):
    rows, d = x2.shape
    hid = wu.shape[1]
    mod_rows = sc.shape[1]
    row_spec = pl.BlockSpec((tm, d), lambda i: (i, 0))
    mod_spec = pl.BlockSpec((1, mod_rows, d), lambda i: (i // tiles_per_mod, 0, 0))
    vec_spec = pl.BlockSpec((1, d), lambda i: (0, 0))
    return pl.pallas_call(
        _mlp_kernel,
        grid=(rows // tm,),
        in_specs=[row_spec, mod_spec, mod_spec, mod_spec, vec_spec, vec_spec,
                  pl.BlockSpec((d, hid), lambda i: (0, 0)),
                  pl.BlockSpec((hid, d), lambda i: (0, 0))],
        out_specs=row_spec,
        out_shape=_sds((rows, d)),
        compiler_params=_cparams(("arbitrary",)),
        name="mlp",
    )(x2, sc, sh, gt, g1, g2, wu, wd)


def _block_diag_tiles(w, per_tile):
    nblk, d, _ = w.shape
    wt = w.reshape(nblk // per_tile, per_tile, d, d)
    eye = jnp.eye(per_tile, dtype=w.dtype)
    bd = wt[:, :, :, None, :] * eye[None, :, None, :, None]
    return bd.reshape(nblk // per_tile, per_tile * d, per_tile * d)


def _layer_weights(l, w_in, w_branch_ssd, w_branch_moba, w_branch_lru, w_out, w_up, w_down,
                   lru_wa, lru_wx):
    o = [0]
    for s in (SSD_D_INNER, SSD_CONV_DIM, SSD_N_HEADS, MOBA_WIDTH, MOBA_WIDTH, MOBA_WIDTH,
              LRU_WIDTH, LRU_WIDTH, N_BRANCH * D_MODEL):
        o.append(o[-1] + s)
    w = w_in[l]
    seg = lambda n: w[:, o[n]:o[n + 1]]
    w_all = jnp.concatenate([seg(1), seg(0), seg(3), seg(4), seg(5), seg(6), seg(7), seg(8)],
                            axis=1).astype(BF16)
    wdt = jnp.pad(seg(2), ((0, 0), (0, DT_PAD - SSD_N_HEADS))).astype(BF16)
    per_tile = 256 // LRU_BLOCK
    return dict(
        w_all=w_all, wdt=wdt,
        ws=w_branch_ssd[l].astype(BF16), wm=w_branch_moba[l].astype(BF16),
        wl=w_branch_lru[l].astype(BF16), wo=w_out[l].astype(BF16),
        wu=w_up[l].astype(BF16), wd=w_down[l].astype(BF16),
        wa_bd=_block_diag_tiles(lru_wa[l], per_tile).astype(BF16),
        wx_bd=_block_diag_tiles(lru_wx[l], per_tile).astype(BF16),
    )


def _row(v):
    return v.reshape(1, -1)


def _pad_lanes(v, width):
    return jnp.pad(v, (0, width - v.shape[0])).reshape(1, width)


def _halo(buf):
    return jnp.pad(buf, ((0, 0), (CONV_HALO - buf.shape[1], 0), (0, 0)))


def _group_layer(x3, mods, wts, prm, ssd_buf, ssd_h, lru_buf, lru_h, moba_fn, *, t_valid, tm_proj,
                 tm_dense, ssd_rows, lru_tm, rows_per_mod):
    bsz, t_pad, d = x3.shape
    rows = bsz * t_pad
    x2 = x3.reshape(rows, d)
    tiles_per_mod = rows_per_mod // tm_dense
    sh1, sc1, gt1, sh2, sc2, gt2 = mods

    p2, dt2 = _inproj_call(x2, sc1, sh1, _row(prm['g_pre_mix']), wts['w_all'], wts['wdt'],
                           tm=tm_proj, tiles_per_mod=rows_per_mod // tm_proj)
    p3 = p2.reshape(bsz, t_pad, P_WIDTH)
    dt3 = dt2.reshape(bsz, t_pad, DT_PAD)

    y_ssd, ssd_h_new = _ssd_call(
        p3, dt3, _halo(ssd_buf), ssd_h.reshape(bsz, SSD_N_HEADS * SSD_HEAD_DIM, SSD_D_STATE),
        prm['ssd_conv_w'], _row(prm['ssd_conv_b']), _pad_lanes(prm['ssd_dt_bias'], DT_PAD),
        _pad_lanes(prm['ssd_a_log'], DT_PAD), _row(jnp.repeat(prm['ssd_d'], SSD_HEAD_DIM)),
        _row(prm['ssd_norm_g']), rows_in=ssd_rows, t_valid=t_valid)

    y_lru, lru_h_new = _lru_call(
        p3, _halo(lru_buf), lru_h.reshape(bsz, 1, LRU_WIDTH), prm['lru_conv_w'],
        _row(prm['lru_conv_b']), wts['wa_bd'], wts['wx_bd'], _row(prm['lru_ba']),
        _row(prm['lru_bx']), _row(prm['lru_lambda']), tm=lru_tm, t_valid=t_valid)

    y_moba = moba_fn(p3)

    x1 = _merge_call(y_ssd.reshape(rows, d), y_moba.reshape(rows, d), y_lru.reshape(rows, d), p2, x2,
                     gt1, _row(prm['g_post_mix']), wts['ws'], wts['wm'], wts['wl'], wts['wo'],
                     tm=tm_dense, tiles_per_mod=tiles_per_mod)
    x_out = _mlp_call(x1, sc2, sh2, gt2, _row(prm['g_pre_mlp']), _row(prm['g_post_mlp']),
                      wts['wu'], wts['wd'], tm=tm_dense, tiles_per_mod=tiles_per_mod)

    lo = t_valid - (SSD_CONV - 1)
    k_new = p3[:, :t_valid, COL_K:COL_K + MOBA_WIDTH].reshape(bsz, t_valid, MOBA_N_HEADS, MOBA_HEAD_DIM)
    v_new = p3[:, :t_valid, COL_V:COL_V + MOBA_WIDTH].reshape(bsz, t_valid, MOBA_N_HEADS, MOBA_HEAD_DIM)
    ssd_buf_new = p3[:, lo:t_valid, COL_XBC:COL_XBC + SSD_CONV_DIM]
    lru_buf_new = p3[:, lo:t_valid, COL_XR:COL_XR + LRU_WIDTH]
    return (x_out.reshape(bsz, t_pad, d), k_new, v_new, ssd_buf_new,
            ssd_h_new.reshape(bsz, SSD_N_HEADS, SSD_HEAD_DIM, SSD_D_STATE), lru_buf_new,
            lru_h_new.reshape(bsz, LRU_WIDTH))


def kernel(x_prompt, x_sample, c_prompt, c_sample, cache_k, cache_v, page_table, state_ssm, state_ssm_conv, state_lru, state_lru_conv, w_mod, b_mod, g_pre_mix, g_post_mix, g_pre_mlp, g_post_mlp, w_in, ssd_conv_w, ssd_conv_b, ssd_dt_bias, ssd_a_log, ssd_d, ssd_norm_g, lru_conv_w, lru_conv_b, lru_wa, lru_ba, lru_wx, lru_bx, lru_lambda, w_branch_ssd, w_branch_moba, w_branch_lru, w_out, w_up, w_down):
    bp, tp, d = x_prompt.shape
    bs, ts, _ = x_sample.shape
    depth = w_in.shape[0]
    n_pool = cache_k.shape[1]
    assert ts <= SAMPLE_T_PAD and ts >= SSD_CONV - 1
    assert (page_table.shape[1] * PAGE_SIZE) % MOBA_BLOCK == 0
    assert tp % SSD_CHUNK == 0 and tp % MOBA_BLOCK == 0

    tm_p = min(1024, tp)
    tm_dense_p = min(256, tp)
    lru_tm_p = min(256, tp)

    n_c = bp + bs
    n_c_pad = -(-n_c // SUBLANES) * SUBLANES
    c_all = jnp.pad(jnp.concatenate([c_prompt, c_sample], axis=0), ((0, n_c_pad - n_c), (0, 0)))
    mod_all = _mod_call(c_all, w_mod, b_mod)

    xs_pad = jnp.pad(x_sample, ((0, 0), (0, SAMPLE_T_PAD - ts), (0, 0)))

    zero_ssd_buf = jnp.zeros((bp, SSD_CONV - 1, SSD_CONV_DIM), F32)
    zero_ssd_h = jnp.zeros((bp, SSD_N_HEADS, SSD_HEAD_DIM, SSD_D_STATE), F32)
    zero_lru_buf = jnp.zeros((bp, LRU_CONV - 1, LRU_WIDTH), F32)
    zero_lru_h = jnp.zeros((bp, LRU_WIDTH), F32)

    outs_p = [[] for _ in range(6)]
    outs_s = [[] for _ in range(6)]
    xp, xs = x_prompt, xs_pad
    for l in range(depth):
        prm = {
            'g_pre_mix': g_pre_mix[l], 'g_post_mix': g_post_mix[l], 'g_pre_mlp': g_pre_mlp[l],
            'g_post_mlp': g_post_mlp[l], 'ssd_conv_w': ssd_conv_w[l], 'ssd_conv_b': ssd_conv_b[l],
            'ssd_dt_bias': ssd_dt_bias[l], 'ssd_a_log': ssd_a_log[l], 'ssd_d': ssd_d[l],
            'ssd_norm_g': ssd_norm_g[l], 'lru_conv_w': lru_conv_w[l], 'lru_conv_b': lru_conv_b[l],
            'lru_ba': lru_ba[l].reshape(-1), 'lru_bx': lru_bx[l].reshape(-1),
            'lru_lambda': lru_lambda[l],
        }
        wts = _layer_weights(l, w_in, w_branch_ssd, w_branch_moba, w_branch_lru, w_out, w_up,
                             w_down, lru_wa, lru_wx)
        mod_l = mod_all[l]
        mods_p = [mod_l[:bp, n * d:(n + 1) * d].reshape(bp, 1, d) for n in range(6)]
        mods_s = [jnp.repeat(mod_l[bp:bp + bs, n * d:(n + 1) * d], SAMPLE_T_PAD, axis=0)
                  .reshape(1, bs * SAMPLE_T_PAD, d) for n in range(6)]

        res_p = _group_layer(xp, mods_p, wts, prm, zero_ssd_buf, zero_ssd_h, zero_lru_buf, zero_lru_h,
                             _moba_prompt_call, t_valid=tp, tm_proj=tm_p, tm_dense=tm_dense_p,
                             ssd_rows=SSD_CHUNK, lru_tm=lru_tm_p, rows_per_mod=tp)
        xp = res_p[0]
        for n in range(6):
            outs_p[n].append(res_p[n + 1])

        def moba_sample(p3, l=l):
            ksum = _ksum_call(page_table, cache_k, l)
            sel = _select_call(p3, ksum)
            sel = sel[:, :ts, :MOBA_N_HEADS * 4].reshape(bs, ts, MOBA_N_HEADS, 4)[..., :MOBA_TOPK]
            return _moba_sample_call(page_table, sel, p3, cache_k, cache_v, l, t_valid=ts)

        res_s = _group_layer(xs, mods_s, wts, prm, state_ssm_conv[l], state_ssm[l], state_lru_conv[l],
                             state_lru[l], moba_sample, t_valid=ts, tm_proj=bs * SAMPLE_T_PAD,
                             tm_dense=bs * SAMPLE_T_PAD, ssd_rows=SAMPLE_T_PAD, lru_tm=SAMPLE_T_PAD,
                             rows_per_mod=bs * SAMPLE_T_PAD)
        xs = res_s[0]
        for n in range(6):
            outs_s[n].append(res_s[n + 1])

    st = lambda lst: jnp.stack(lst)
    kp, vp, ssmc_p, ssm_p, lruc_p, lru_p = outs_p
    ks, vs, ssmc_s, ssm_s, lruc_s, lru_s = outs_s
    return (xp, xs[:, :ts],
            st(kp), st(vp), st(ks), st(vs),
            st(ssm_p), st(ssmc_p), st(lru_p), st(lruc_p),
            st(ssm_s), st(ssmc_s), st(lru_s), st(lruc_s))
```

```python
import functools
import math

import jax
import jax.numpy as jnp
from jax import lax
from jax.experimental import pallas as pl
from jax.experimental.pallas import tpu as pltpu

F32 = jnp.float32
BF16 = jnp.bfloat16
HIGHEST = lax.Precision.HIGHEST

D_MODEL = 1024
PAGE_SIZE = 128
SSD_D_INNER = D_MODEL
SSD_HEAD_DIM = 64
SSD_N_HEADS = SSD_D_INNER // SSD_HEAD_DIM
SSD_N_GROUPS = 4
SSD_D_STATE = 128
SSD_CONV = 4
SSD_CHUNK = 128
SSD_CONV_DIM = SSD_D_INNER + 2 * SSD_N_GROUPS * SSD_D_STATE
MOBA_N_HEADS = 8
MOBA_HEAD_DIM = 128
MOBA_WIDTH = MOBA_N_HEADS * MOBA_HEAD_DIM
MOBA_BLOCK = 256
MOBA_TOPK = 3
LRU_WIDTH = D_MODEL
LRU_N_BLOCKS = 16
LRU_BLOCK = LRU_WIDTH // LRU_N_BLOCKS
LRU_CONV = 4
LRU_C = 8.0
N_BRANCH = 3
MLP_HIDDEN = 4 * D_MODEL
RMS_EPS = 1e-6

SUBLANES = 8
LANES = 128
VMEM_LIMIT_BYTES = 56 * 1024 * 1024

SAMPLE_T_PAD = SUBLANES
CONV_HALO = SUBLANES
NEG_BIG = -1e30
LOG2E = math.log2(math.e)

COL_XBC = 0
COL_Z = COL_XBC + SSD_CONV_DIM
COL_Q = COL_Z + SSD_D_INNER
COL_XR = COL_Q + MOBA_WIDTH
COL_GR = COL_XR + LRU_WIDTH
COL_GATE = COL_GR + LRU_WIDTH
P_WIDTH = COL_GATE + N_BRANCH * D_MODEL
PROJ_TN = 1024
P_TILES = P_WIDTH // PROJ_TN
DT_PAD = LANES


def _cparams(sem):
    return pltpu.CompilerParams(dimension_semantics=sem, vmem_limit_bytes=VMEM_LIMIT_BYTES)


def _sds(shape, dtype=F32):
    return jax.ShapeDtypeStruct(shape, dtype)


def _silu(x):
    return x * jax.nn.sigmoid(x)


def _softplus(x):
    return jnp.maximum(x, 0.0) + jnp.log1p(jnp.exp(-jnp.abs(x)))


def _rms(x):
    return x * lax.rsqrt(jnp.mean(x * x, axis=-1, keepdims=True) + RMS_EPS)


def _dot(a, b):
    return jnp.dot(a, b, preferred_element_type=F32)


def _dot_nt(a, b, precision=None):
    return lax.dot_general(a, b, (((1,), (1,)), ((), ())), precision=precision,
                           preferred_element_type=F32)


def _mod_kernel(c_ref, w_ref, b_ref, o_ref):
    s = _silu(c_ref[...]).astype(BF16)
    o_ref[0] = _dot(s, w_ref[0].astype(BF16)) + b_ref[0]


def _mod_call(c_all, w_mod, b_mod):
    depth, d, n = w_mod.shape
    rows = c_all.shape[0]
    tn = 1536
    return pl.pallas_call(
        _mod_kernel,
        grid=(depth, n // tn),
        in_specs=[pl.BlockSpec((rows, d), lambda l, j: (0, 0)),
                  pl.BlockSpec((1, d, tn), lambda l, j: (l, 0, j)),
                  pl.BlockSpec((1, 1, tn), lambda l, j: (l, 0, j))],
        out_specs=pl.BlockSpec((1, rows, tn), lambda l, j: (l, 0, j)),
        out_shape=_sds((depth, rows, n)),
        compiler_params=_cparams(("arbitrary", "arbitrary")),
        name="mod",
    )(c_all, w_mod, b_mod.reshape(depth, 1, n))


def _inproj_kernel(x_ref, sc_ref, sh_ref, g_ref, w_ref, wdt_ref, p_ref, dt_ref, k_ref, v_ref, h_scr):
    j = pl.program_id(1)

    @pl.when(j == 0)
    def _():
        y = _rms(x_ref[...]) * g_ref[...]
        hb = (y * (1.0 + sc_ref[0]) + sh_ref[0]).astype(BF16)
        h_scr[...] = hb
        dt_ref[...] = _dot(hb, wdt_ref[...])

    @pl.when(j < P_TILES)
    def _():
        p_ref[...] = _dot(h_scr[...], w_ref[...])

    @pl.when(j == P_TILES)
    def _():
        k_ref[...] = _dot(h_scr[...], w_ref[...])

    @pl.when(j == P_TILES + 1)
    def _():
        v_ref[...] = _dot(h_scr[...], w_ref[...])


def _inproj_call(x2, sc, sh, g, w_all, wdt, *, tm, tiles_per_mod):
    rows, d = x2.shape
    assert PROJ_TN == MOBA_WIDTH and w_all.shape[1] == P_WIDTH + 2 * MOBA_WIDTH
    mod_rows = sc.shape[1]
    mod_spec = pl.BlockSpec((1, mod_rows, d), lambda i, j: (i // tiles_per_mod, 0, 0))
    kv_spec = pl.BlockSpec((tm, MOBA_WIDTH), lambda i, j: (i, 0))
    return pl.pallas_call(
        _inproj_kernel,
        grid=(rows // tm, P_TILES + 2),
        in_specs=[pl.BlockSpec((tm, d), lambda i, j: (i, 0)),
                  mod_spec, mod_spec,
                  pl.BlockSpec((1, d), lambda i, j: (0, 0)),
                  pl.BlockSpec((d, PROJ_TN), lambda i, j: (0, j)),
                  pl.BlockSpec((d, DT_PAD), lambda i, j: (0, 0))],
        out_specs=[pl.BlockSpec((tm, PROJ_TN), lambda i, j: (i, jnp.minimum(j, P_TILES - 1))),
                   pl.BlockSpec((tm, DT_PAD), lambda i, j: (i, 0)),
                   kv_spec, kv_spec],
        out_shape=[_sds((rows, P_WIDTH)), _sds((rows, DT_PAD)),
                   _sds((rows, MOBA_WIDTH)), _sds((rows, MOBA_WIDTH))],
        scratch_shapes=[pltpu.VMEM((tm, d), BF16)],
        compiler_params=_cparams(("arbitrary", "arbitrary")),
        name="inproj",
    )(x2, sc, sh, g, w_all, wdt)


def _conv_tile(prev_scr, x, buf_ref, w_ref, b_ref, tile_idx):
    rows, width = x.shape

    @pl.when(tile_idx == 0)
    def _():
        prev_scr[...] = buf_ref[0]

    prev = prev_scr[...]
    prev_scr[...] = x[rows - CONV_HALO:, :]
    row = lax.broadcasted_iota(jnp.int32, (CONV_HALO, width), 0)
    acc = b_ref[...] + w_ref[SSD_CONV - 1:SSD_CONV, :] * x
    for k in range(1, SSD_CONV):
        rolled = pltpu.roll(x, k, 0)
        head = jnp.where(row >= k, rolled[0:CONV_HALO, :], pltpu.roll(prev, k, 0))
        shifted = head if rows == CONV_HALO else jnp.concatenate([head, rolled[CONV_HALO:, :]], axis=0)
        acc = acc + w_ref[SSD_CONV - 1 - k:SSD_CONV - k, :] * shifted
    return acc


def _ssd_kernel(xbc_ref, z_ref, dt_ref, buf_ref, h0_ref, cw_ref, cb_ref, dtb_ref, alog_ref,
                dsk_ref, ng_ref, y_ref, h_ref, xp_scr, dt_scr, *, rows_in, t_valid):
    L = SSD_CHUNK
    N = SSD_D_STATE
    P2 = 2 * SSD_HEAD_DIM
    c = pl.program_id(1)

    @pl.when(c == 0)
    def _():
        h_ref[0] = h0_ref[0]

    xc = _silu(_conv_tile(xp_scr, xbc_ref[0], buf_ref, cw_ref, cb_ref, c))
    if rows_in < L:
        xc = jnp.concatenate([xc, jnp.zeros((L - rows_in, SSD_CONV_DIM), F32)], axis=0)
    xs = xc[:, :SSD_D_INNER]
    bm = xc[:, SSD_D_INNER:SSD_D_INNER + SSD_N_GROUPS * N]
    cm = xc[:, SSD_D_INNER + SSD_N_GROUPS * N:]

    if rows_in < L:
        dt_scr[...] = jnp.zeros((L, LANES), F32)
    dt_scr[0:rows_in, :] = dt_ref[0]
    lane = lax.broadcasted_iota(jnp.int32, (L, LANES), 1)
    row = lax.broadcasted_iota(jnp.int32, (L, LANES), 0)
    dtv = _softplus(dt_scr[...] + dtb_ref[...])
    a = jnp.where(lane < SSD_N_HEADS, -jnp.exp(alog_ref[...]), 0.0)
    dtv = jnp.where((row + c * L < t_valid) & (lane < SSD_N_HEADS), dtv, 0.0)
    dta = dtv * a
    ri = lax.broadcasted_iota(jnp.int32, (L, L), 0)
    ci = lax.broadcasted_iota(jnp.int32, (L, L), 1)
    causal = ri >= ci
    cum = jnp.dot(causal.astype(F32), dta, precision=HIGHEST, preferred_element_type=F32)
    cum_t = cum.T
    dtv_t = dtv.T
    ecum = jnp.exp(cum)
    cum_last = cum[L - 1:L, :]
    wgt = jnp.exp(cum_last - cum) * dtv
    cdec_t = jnp.exp(cum_t[:, L - 1:L])
    left = lax.broadcasted_iota(jnp.int32, (L, P2), 1) < SSD_HEAD_DIM
    left_rows = lax.broadcasted_iota(jnp.int32, (P2, N), 0) < SSD_HEAD_DIM

    y_tiles = []
    for j in range(SSD_N_HEADS // 2):
        g = j // (SSD_N_HEADS // SSD_N_GROUPS // 2)
        h_a, h_b = 2 * j, 2 * j + 1
        bm_g = bm[:, g * N:(g + 1) * N]
        cm_g = cm[:, g * N:(g + 1) * N].astype(BF16)
        cb = _dot_nt(cm_g, bm_g.astype(BF16))
        xs_p = xs[:, j * P2:(j + 1) * P2]
        m_parts = []
        for hh in (h_a, h_b):
            seg = cum[:, hh:hh + 1] - cum_t[hh:hh + 1, :]
            dec = jnp.exp(jnp.where(causal, seg, -jnp.inf))
            m_parts.append((cb * dec * dtv_t[hh:hh + 1, :]).astype(BF16))
        m_pair = jnp.concatenate(m_parts, axis=1)
        x_bd = jnp.concatenate([jnp.where(left, xs_p, 0.0), jnp.where(left, 0.0, xs_p)],
                               axis=0).astype(BF16)
        y_diag = _dot(m_pair, x_bd)
        h_prev = h_ref[0, j * P2:(j + 1) * P2, :]
        y_off = _dot_nt(cm_g, h_prev.astype(BF16))
        e_sel = jnp.where(left, ecum[:, h_a:h_a + 1], ecum[:, h_b:h_b + 1])
        y_tiles.append(y_diag + y_off * e_sel + dsk_ref[:, j * P2:(j + 1) * P2] * xs_p)
        w_sel = jnp.where(left, wgt[:, h_a:h_a + 1], wgt[:, h_b:h_b + 1])
        xw_t = (xs_p * w_sel).T.astype(BF16)
        st = _dot(xw_t, bm_g.astype(BF16))
        d_sel = jnp.where(left_rows, cdec_t[h_a:h_a + 1, :], cdec_t[h_b:h_b + 1, :])
        h_ref[0, j * P2:(j + 1) * P2, :] = h_prev * d_sel + st

    gw = SSD_D_INNER // SSD_N_GROUPS
    tiles_per_group = gw // P2
    zs = _silu(z_ref[0])
    for g in range(SSD_N_GROUPS):
        yg = jnp.concatenate(y_tiles[g * tiles_per_group:(g + 1) * tiles_per_group], axis=1)
        yg = yg[0:rows_in, :] * zs[:, g * gw:(g + 1) * gw]
        y_ref[0, :, g * gw:(g + 1) * gw] = (_rms(yg) * ng_ref[:, g * gw:(g + 1) * gw]).astype(BF16)


def _ssd_call(p3, dt3, buf8, h0, cw, cb, dtb, alog, dsk, ng, *, rows_in, t_valid):
    bsz, t_pad, _ = p3.shape
    nc = t_pad // rows_in
    L = SSD_CHUNK
    hp = SSD_N_HEADS * SSD_HEAD_DIM
    const2 = lambda b, c: (0, 0)
    kern = functools.partial(_ssd_kernel, rows_in=rows_in, t_valid=t_valid)
    return pl.pallas_call(
        kern,
        grid=(bsz, nc),
        in_specs=[pl.BlockSpec((1, rows_in, SSD_CONV_DIM), lambda b, c: (b, c, COL_XBC // SSD_CONV_DIM)),
                  pl.BlockSpec((1, rows_in, SSD_D_INNER), lambda b, c: (b, c, COL_Z // SSD_D_INNER)),
                  pl.BlockSpec((1, rows_in, DT_PAD), lambda b, c: (b, c, 0)),
                  pl.BlockSpec((1, CONV_HALO, SSD_CONV_DIM), lambda b, c: (b, 0, 0)),
                  pl.BlockSpec((1, hp, SSD_D_STATE), lambda b, c: (b, 0, 0)),
                  pl.BlockSpec((SSD_CONV, SSD_CONV_DIM), const2),
                  pl.BlockSpec((1, SSD_CONV_DIM), const2),
                  pl.BlockSpec((1, DT_PAD), const2),
                  pl.BlockSpec((1, DT_PAD), const2),
                  pl.BlockSpec((1, SSD_D_INNER), const2),
                  pl.BlockSpec((1, SSD_D_INNER), const2)],
        out_specs=[pl.BlockSpec((1, rows_in, SSD_D_INNER), lambda b, c: (b, c, 0)),
                   pl.BlockSpec((1, hp, SSD_D_STATE), lambda b, c: (b, 0, 0))],
        out_shape=[_sds((bsz, t_pad, SSD_D_INNER), BF16), _sds((bsz, hp, SSD_D_STATE))],
        scratch_shapes=[pltpu.VMEM((CONV_HALO, SSD_CONV_DIM), F32),
                        pltpu.VMEM((L, DT_PAD), F32)],
        compiler_params=_cparams(("arbitrary", "arbitrary")),
        name="ssd",
    )(p3, p3, dt3, buf8, h0, cw, cb, dtb, alog, dsk, ng)


def _lru_kernel(xr_ref, gr_ref, buf_ref, h0_ref, cw_ref, cb_ref, wa_ref, wx_ref, ba_ref, bx_ref,
                lam_ref, y_ref, h_ref, xp_scr, *, tm, t_valid, t_pad):
    i = pl.program_id(1)

    @pl.when(i == 0)
    def _():
        h_ref[0] = h0_ref[0]

    xc = _conv_tile(xp_scr, xr_ref[0], buf_ref, cw_ref, cb_ref, i)
    bw = wa_ref.shape[1]
    r_parts, i_parts = [], []
    for j in range(LRU_WIDTH // bw):
        xj = xc[:, j * bw:(j + 1) * bw].astype(BF16)
        r_parts.append(_dot(xj, wa_ref[j]))
        i_parts.append(_dot(xj, wx_ref[j]))
    r = jax.nn.sigmoid(jnp.concatenate(r_parts, axis=1) + ba_ref[...])
    ig = jax.nn.sigmoid(jnp.concatenate(i_parts, axis=1) + bx_ref[...])
    log_a = (-LRU_C) * r * _softplus(-lam_ref[...])
    a_full = jnp.exp(log_a)
    b_full = jnp.sqrt(-jnp.tanh(log_a) * (a_full * a_full + 1.0)) * (ig * xc)
    gate = jax.nn.gelu(gr_ref[0], approximate=True)

    row = lax.broadcasted_iota(jnp.int32, (tm, LANES), 0)
    row_in_group = row % SUBLANES
    for s in range(LRU_WIDTH // LANES):
        sl = slice(s * LANES, (s + 1) * LANES)
        a, b = a_full[:, sl], b_full[:, sl]
        if t_valid < t_pad:
            valid = row + i * tm < t_valid
            a = jnp.where(valid, a, 1.0)
            b = jnp.where(valid, b, 0.0)
        k = 1
        while k < SUBLANES:
            keep = row_in_group >= k
            b = b + a * jnp.where(keep, pltpu.roll(b, k, 0), 0.0)
            a = a * jnp.where(keep, pltpu.roll(a, k, 0), 1.0)
            k *= 2
        carry = h_ref[0, :, sl]
        groups = []
        for g in range(tm // SUBLANES):
            rs = slice(g * SUBLANES, (g + 1) * SUBLANES)
            hg = b[rs, :] + a[rs, :] * carry
            groups.append(hg)
            carry = hg[SUBLANES - 1:SUBLANES, :]
        h = groups[0] if len(groups) == 1 else jnp.concatenate(groups, axis=0)
        y_ref[0, :, sl] = (h * gate[:, sl]).astype(BF16)
        h_ref[0, :, sl] = carry


def _lru_call(p3, buf8, h0, cw, cb, wa_bd, wx_bd, ba, bx, lam, *, tm, t_valid):
    bsz, t_pad, _ = p3.shape
    W = LRU_WIDTH
    nb, bw, _ = wa_bd.shape
    const2 = lambda b, i: (0, 0)
    const3 = lambda b, i: (0, 0, 0)
    kern = functools.partial(_lru_kernel, tm=tm, t_valid=t_valid, t_pad=t_pad)
    return pl.pallas_call(
        kern,
        grid=(bsz, t_pad // tm),
        in_specs=[pl.BlockSpec((1, tm, W), lambda b, i: (b, i, COL_XR // W)),
                  pl.BlockSpec((1, tm, W), lambda b, i: (b, i, COL_GR // W)),
                  pl.BlockSpec((1, CONV_HALO, W), lambda b, i: (b, 0, 0)),
                  pl.BlockSpec((1, 1, W), lambda b, i: (b, 0, 0)),
                  pl.BlockSpec((LRU_CONV, W), const2),
                  pl.BlockSpec((1, W), const2),
                  pl.BlockSpec((nb, bw, bw), const3),
                  pl.BlockSpec((nb, bw, bw), const3),
                  pl.BlockSpec((1, W), const2),
                  pl.BlockSpec((1, W), const2),
                  pl.BlockSpec((1, W), const2)],
        out_specs=[pl.BlockSpec((1, tm, W), lambda b, i: (b, i, 0)),
                   pl.BlockSpec((1, 1, W), lambda b, i: (b, 0, 0))],
        out_shape=[_sds((bsz, t_pad, W), BF16), _sds((bsz, 1, W))],
        scratch_shapes=[pltpu.VMEM((CONV_HALO, W), F32)],
        compiler_params=_cparams(("arbitrary", "arbitrary")),
        name="lru",
    )(p3, p3, buf8, h0, cw, cb, wa_bd, wx_bd, ba, bx, lam)


def _topk_mask(gate, n_valid, col):
    nb = gate.shape[1]
    past = col < n_valid
    gm = jnp.where(past, gate, -jnp.inf)
    rank = jnp.zeros(gate.shape, F32)
    for j in range(nb):
        gj = gm[:, j:j + 1]
        ahead = (gj > gm) | ((gj == gm) & (col > j))
        rank = rank + jnp.where(ahead, 1.0, 0.0)
    return (rank < MOBA_TOPK) & past, rank


def _moba_prompt_kernel(q_ref, k_ref, v_ref, o_ref, kmean_scr, kg_scr, vtg_scr, sel_scr, *, nb, grp, nh):
    i = pl.program_id(2)
    BS = MOBA_BLOCK
    GB = grp * BS
    dh = MOBA_HEAD_DIM
    scale = MOBA_HEAD_DIM ** -0.5

    @pl.when(i == 0)
    def _():
        for hh in range(nh):
            hs = slice(hh * dh, (hh + 1) * dh)
            for j in range(nb):
                kmean_scr[hh, j:j + 1, :] = jnp.mean(k_ref[0, j * BS:(j + 1) * BS, hs], axis=0,
                                                     keepdims=True)
            for jj in range(nb // grp):
                kg_scr[hh, jj] = k_ref[0, jj * GB:(jj + 1) * GB, hs].astype(BF16)
                for u in range(grp):
                    r0 = jj * GB + u * BS
                    vtg_scr[hh, jj, :, u * BS:(u + 1) * BS] = v_ref[0, r0:r0 + BS, hs].T.astype(BF16)

    qb_t = []
    for hh in range(nh):
        q_t = q_ref[0, :, hh * dh:(hh + 1) * dh].T
        qb_t.append(q_t.astype(BF16))
        gate = jnp.dot(kmean_scr[hh], q_t, precision=HIGHEST, preferred_element_type=F32)
        blk = lax.broadcasted_iota(jnp.int32, gate.shape, 0)
        past = blk < i
        gm = jnp.where(past, gate, -jnp.inf)
        rank = jnp.zeros(gate.shape, F32)
        for j in range(nb):
            gj = gm[j:j + 1, :]
            ahead = (gj > gm) | ((gj == gm) & (blk > j))
            rank = rank + jnp.where(ahead, 1.0, 0.0)
        sel = jnp.where((rank < MOBA_TOPK) & past, 0.0, -jnp.inf)
        for j in range(nb):
            sel_scr[hh, j] = jnp.broadcast_to(sel[j:j + 1, :], (SUBLANES, BS))

    def block_bias(hh, j):
        rows = sel_scr[hh, j]
        return jnp.broadcast_to(rows[None], (BS // SUBLANES, SUBLANES, BS)).reshape(BS, BS)

    sub = min(MOBA_SUB, grp)
    n_parts = grp // sub
    SB = sub * BS

    def scores(hh, jj, part):
        return _dot(kg_scr[hh, jj, part * SB:(part + 1) * SB, :], qb_t[hh])

    def update(carry, hh, jj, part, s_raw, bias):
        m, l, acc = carry
        s = s_raw * (scale * LOG2E) + bias
        m_new = jnp.maximum(m, jnp.max(s, axis=0, keepdims=True))
        alpha = jnp.exp2(m - m_new)
        p = jnp.exp2(s - m_new)
        l_new = alpha * l + jnp.sum(p, axis=0, keepdims=True)
        v_t = vtg_scr[hh, jj, :, part * SB:(part + 1) * SB]
        acc_new = alpha * acc + _dot(v_t, p.astype(BF16))
        return m_new, l_new, acc_new

    def run_parts(jj, carries, bias_fn, parts):
        s_raw = {part: [scores(hh, jj, part) for hh in range(nh)] for part in parts}
        carries = list(carries)
        for part in parts:
            for hh in range(nh):
                bias = jnp.concatenate(
                    [bias_fn(hh, jj * grp + part * sub + u) for u in range(sub)], axis=0)
                carries[hh] = update(carries[hh], hh, jj, part, s_raw[part][hh], bias)
        return tuple(carries)

    def body(jj, carries):
        return run_parts(jj, carries, block_bias, range(n_parts))

    init = tuple((jnp.full((1, BS), NEG_BIG, F32), jnp.zeros((1, BS), F32), jnp.zeros((dh, BS), F32))
                 for _ in range(nh))
    last = i // grp
    carries = lax.fori_loop(0, last, body, init)

    key = lax.broadcasted_iota(jnp.int32, (BS, BS), 0)
    qry = lax.broadcasted_iota(jnp.int32, (BS, BS), 1)
    causal_bias = jnp.where(key <= qry, 0.0, -jnp.inf)

    def last_bias(hh, j):
        return jnp.where(j == i, causal_bias, block_bias(hh, j))

    carries = run_parts(last, carries, last_bias, [0])
    for part in range(1, n_parts):
        carries = lax.cond(last * grp + part * sub <= i,
                           lambda c, part=part: run_parts(last, c, last_bias, [part]),
                           lambda c: c, carries)
    for hh in range(nh):
        m, l, acc = carries[hh]
        o_ref[0, :, hh * dh:(hh + 1) * dh] = (acc / l).T.astype(BF16)


MOBA_GROUP = 4
MOBA_SUB = 2
MOBA_HEADS_PER_STEP = 2


def _moba_prompt_call(p3, k3, v3):
    bsz, t, _ = p3.shape
    assert t % MOBA_BLOCK == 0 and MOBA_N_HEADS % MOBA_HEADS_PER_STEP == 0
    nb = t // MOBA_BLOCK
    grp = math.gcd(MOBA_GROUP, nb)
    nh = MOBA_HEADS_PER_STEP
    dh = MOBA_HEAD_DIM
    hw = nh * dh
    kern = functools.partial(_moba_prompt_kernel, nb=nb, grp=grp, nh=nh)
    return pl.pallas_call(
        kern,
        grid=(bsz, MOBA_N_HEADS // nh, nb),
        in_specs=[pl.BlockSpec((1, MOBA_BLOCK, hw), lambda b, h, i: (b, i, COL_Q // hw + h)),
                  pl.BlockSpec((1, t, hw), lambda b, h, i: (b, 0, h)),
                  pl.BlockSpec((1, t, hw), lambda b, h, i: (b, 0, h))],
        out_specs=pl.BlockSpec((1, MOBA_BLOCK, hw), lambda b, h, i: (b, i, h)),
        out_shape=_sds((bsz, t, MOBA_WIDTH), BF16),
        scratch_shapes=[pltpu.VMEM((nh, nb, dh), F32),
                        pltpu.VMEM((nh, nb // grp, grp * MOBA_BLOCK, dh), BF16),
                        pltpu.VMEM((nh, nb // grp, dh, grp * MOBA_BLOCK), BF16),
                        pltpu.VMEM((nh, nb, SUBLANES, MOBA_BLOCK), F32)],
        compiler_params=_cparams(("arbitrary", "arbitrary", "arbitrary")),
        name="moba_prompt",
    )(p3, k3, v3)


PAGES_PER_BLOCK = MOBA_BLOCK // PAGE_SIZE
BLOCKS_PER_STEP = SUBLANES
PAGES_PER_STEP = BLOCKS_PER_STEP * PAGES_PER_BLOCK


def _ksum_kernel(pt_ref, *refs):
    page_refs, o_ref = refs[:PAGES_PER_STEP], refs[PAGES_PER_STEP]
    for j in range(BLOCKS_PER_STEP):
        acc = jnp.sum(page_refs[j * PAGES_PER_BLOCK][0, 0], axis=0)
        for u in range(1, PAGES_PER_BLOCK):
            acc = acc + jnp.sum(page_refs[j * PAGES_PER_BLOCK + u][0, 0], axis=0)
        for h in range(MOBA_N_HEADS):
            o_ref[0, h, j:j + 1, :] = acc[h:h + 1, :]


def _ksum_call(page_table, cache, layer):
    bs, n_pages = page_table.shape
    assert n_pages % PAGES_PER_STEP == 0

    def page_spec(p):
        return pl.BlockSpec((1, 1, PAGE_SIZE, MOBA_N_HEADS, MOBA_HEAD_DIM),
                            lambda b, g, pt: (layer, pt[b, g * PAGES_PER_STEP + p], 0, 0, 0))

    grid_spec = pltpu.PrefetchScalarGridSpec(
        num_scalar_prefetch=1,
        grid=(bs, n_pages // PAGES_PER_STEP),
        in_specs=[page_spec(p) for p in range(PAGES_PER_STEP)],
        out_specs=pl.BlockSpec((1, MOBA_N_HEADS, BLOCKS_PER_STEP, MOBA_HEAD_DIM),
                               lambda b, g, pt: (b, 0, g, 0)),
    )
    return pl.pallas_call(
        _ksum_kernel,
        grid_spec=grid_spec,
        out_shape=_sds((bs, MOBA_N_HEADS, n_pages // PAGES_PER_BLOCK, MOBA_HEAD_DIM)),
        compiler_params=_cparams(("arbitrary", "arbitrary")),
        name="moba_ksum",
    )(page_table, *([cache] * PAGES_PER_STEP))


def _select_kernel(q_ref, ks_ref, o_ref, *, nb):
    rows = q_ref.shape[1]
    col = lax.broadcasted_iota(jnp.int32, (rows, nb), 1)
    colf = col.astype(F32)
    lane = lax.broadcasted_iota(jnp.int32, (rows, LANES), 1)
    out = jnp.zeros((rows, LANES), F32)
    for h in range(MOBA_N_HEADS):
        sl = slice(h * MOBA_HEAD_DIM, (h + 1) * MOBA_HEAD_DIM)
        kmean = ks_ref[0, h] * (1.0 / MOBA_BLOCK)
        gate = _dot_nt(q_ref[0, :, sl], kmean, precision=HIGHEST)
        sel, rank = _topk_mask(gate, nb, col)
        for r in range(MOBA_TOPK):
            idx = jnp.sum(jnp.where(sel & (rank == r), colf, 0.0), axis=1, keepdims=True)
            out = jnp.where(lane == h * 4 + r, idx, out)
    o_ref[0] = out.astype(jnp.int32)


def _select_call(p3, ksum):
    bs, rows, _ = p3.shape
    nb = ksum.shape[2]
    assert nb >= MOBA_TOPK
    return pl.pallas_call(
        functools.partial(_select_kernel, nb=nb),
        grid=(bs,),
        in_specs=[pl.BlockSpec((1, rows, MOBA_WIDTH), lambda b: (b, 0, COL_Q // MOBA_WIDTH)),
                  pl.BlockSpec((1, MOBA_N_HEADS, nb, MOBA_HEAD_DIM), lambda b: (b, 0, 0, 0))],
        out_specs=pl.BlockSpec((1, rows, LANES), lambda b: (b, 0, 0)),
        out_shape=_sds((bs, rows, LANES), jnp.int32),
        compiler_params=_cparams(("arbitrary",)),
        name="moba_select",
    )(p3, ksum)


N_SEL_PAGES = MOBA_TOPK * PAGES_PER_BLOCK


def _moba_sample_kernel(pt_ref, sel_ref, q_ref, kn_ref, vn_ref, ck_hbm, cv_hbm, o_ref,
                        kbuf, vbuf, sem, *, layer, t_valid):
    step = pl.program_id(0)
    n_steps = pl.num_programs(0)
    t = step % t_valid
    slot = step % 2
    rows = q_ref.shape[1]
    dh = MOBA_HEAD_DIM
    scale = MOBA_HEAD_DIM ** -0.5

    def gather(st, sl, start):
        b = st // t_valid
        for h in range(MOBA_N_HEADS):
            for r in range(MOBA_TOPK):
                blk = sel_ref[(st * MOBA_N_HEADS + h) * MOBA_TOPK + r] if start else 0
                for half in range(PAGES_PER_BLOCK):
                    page = pt_ref[b, blk * PAGES_PER_BLOCK + half] if start else 0
                    n = r * PAGES_PER_BLOCK + half
                    for src, dst in ((ck_hbm, kbuf), (cv_hbm, vbuf)):
                        cp = pltpu.make_async_copy(src.at[layer, page, :, h, :], dst.at[sl, h, n],
                                                   sem.at[sl])
                        if start:
                            cp.start()
                        else:
                            cp.wait()

    @pl.when(step == 0)
    def _():
        gather(step, slot, True)

    @pl.when(step + 1 < n_steps)
    def _():
        gather(step + 1, 1 - slot, True)

    gather(step, slot, False)

    @pl.when(t == 0)
    def _():
        o_ref[...] = jnp.zeros(o_ref.shape, F32)

    u = lax.broadcasted_iota(jnp.int32, (rows, dh), 0)
    own_ok = (u <= t) & (u < t_valid)
    is_t = u == t
    for h in range(MOBA_N_HEADS):
        sl = slice(h * dh, (h + 1) * dh)
        q_row = jnp.sum(jnp.where(is_t, q_ref[0, :, sl], 0.0), axis=0, keepdims=True)
        q_rep = jnp.broadcast_to(q_row, (dh, dh)).astype(BF16)
        s_own = jnp.where(own_ok, _dot_nt(kn_ref[0, :, sl].astype(BF16), q_rep) * scale, NEG_BIG)
        s_parts = [_dot_nt(kbuf[slot, h, n].astype(BF16), q_rep) * scale for n in range(N_SEL_PAGES)]
        m = jnp.max(s_own, axis=0, keepdims=True)
        for sp in s_parts:
            m = jnp.maximum(m, jnp.max(sp, axis=0, keepdims=True))
        p_own = jnp.exp(s_own - m)
        l = jnp.sum(p_own, axis=0, keepdims=True)
        acc = jnp.sum(p_own * vn_ref[0, :, sl], axis=0, keepdims=True)
        for n, sp in enumerate(s_parts):
            pp = jnp.exp(sp - m)
            l = l + jnp.sum(pp, axis=0, keepdims=True)
            acc = acc + jnp.sum(pp * vbuf[slot, h, n], axis=0, keepdims=True)
        o_ref[0, :, sl] = jnp.where(is_t, acc / l, o_ref[0, :, sl])


def _moba_sample_call(page_table, sel, p3, k3, v3, cache_k, cache_v, layer, *, t_valid):
    bs, rows, _ = p3.shape
    W = MOBA_WIDTH

    def new_spec(col):
        return pl.BlockSpec((1, rows, W), lambda s, pt, sl: (s // t_valid, 0, col // W))

    buf_shape = (2, MOBA_N_HEADS, N_SEL_PAGES, PAGE_SIZE, MOBA_HEAD_DIM)
    grid_spec = pltpu.PrefetchScalarGridSpec(
        num_scalar_prefetch=2,
        grid=(bs * t_valid,),
        in_specs=[new_spec(COL_Q), new_spec(0), new_spec(0),
                  pl.BlockSpec(memory_space=pl.ANY), pl.BlockSpec(memory_space=pl.ANY)],
        out_specs=pl.BlockSpec((1, rows, W), lambda s, pt, sl: (s // t_valid, 0, 0)),
        scratch_shapes=[pltpu.VMEM(buf_shape, F32), pltpu.VMEM(buf_shape, F32),
                        pltpu.SemaphoreType.DMA((2,))],
    )
    return pl.pallas_call(
        functools.partial(_moba_sample_kernel, layer=layer, t_valid=t_valid),
        grid_spec=grid_spec,
        out_shape=_sds((bs, rows, W)),
        compiler_params=_cparams(("arbitrary",)),
        name="moba_sample",
    )(page_table, sel.reshape(-1), p3, k3, v3, cache_k, cache_v)


def _merge_kernel(ys_ref, ym_ref, yl_ref, g0_ref, g1_ref, g2_ref, x_ref, gt_ref, gp_ref,
                  ws_ref, wm_ref, wl_ref, wo_ref, o_ref):
    merged = (jax.nn.sigmoid(g0_ref[...]) * _dot(ys_ref[...].astype(BF16), ws_ref[...])
              + jax.nn.sigmoid(g1_ref[...]) * _dot(ym_ref[...].astype(BF16), wm_ref[...])
              + jax.nn.sigmoid(g2_ref[...]) * _dot(yl_ref[...].astype(BF16), wl_ref[...]))
    mix = _dot(merged.astype(BF16), wo_ref[...])
    o_ref[...] = x_ref[...] + gt_ref[0] * (_rms(mix) * gp_ref[...])


def _merge_call(ys, ym, yl, p2, x2, gt, gp, ws, wm, wl, wo, *, tm, tiles_per_mod):
    rows, d = x2.shape
    mod_rows = gt.shape[1]
    row_spec = pl.BlockSpec((tm, d), lambda i: (i, 0))
    w_spec = pl.BlockSpec((d, d), lambda i: (0, 0))

    def gate_spec(n):
        return pl.BlockSpec((tm, d), lambda i: (i, COL_GATE // d + n))

    return pl.pallas_call(
        _merge_kernel,
        grid=(rows // tm,),
        in_specs=[row_spec, row_spec, row_spec, gate_spec(0), gate_spec(1), gate_spec(2), row_spec,
                  pl.BlockSpec((1, mod_rows, d), lambda i: (i // tiles_per_mod, 0, 0)),
                  pl.BlockSpec((1, d), lambda i: (0, 0)),
                  w_spec, w_spec, w_spec, w_spec],
        out_specs=row_spec,
        out_shape=_sds((rows, d)),
        compiler_params=_cparams(("arbitrary",)),
        name="merge",
    )(ys, ym, yl, p2, p2, p2, x2, gt, gp, ws, wm, wl, wo)


def _mlp_kernel(x_ref, sc_ref, sh_ref, gt_ref, g1_ref, g2_ref, wu_ref, wd_ref, o_ref):
    x = x_ref[...]
    h2 = (_rms(x) * g1_ref[...]) * (1.0 + sc_ref[0]) + sh_ref[0]
    up = jnp.maximum(_dot(h2.astype(BF16), wu_ref[...]), 0.0)
    f = _dot((up * up).astype(BF16), wd_ref[...])
    o_ref[...] = x + gt_ref[0] * (_rms(f) * g2_ref[...])


def _mlp_call(x2, sc, sh, gt, g1, g2, wu, wd, *, tm, tiles_per_mod):
    rows, d = x2.shape
    hid = wu.shape[1]
    mod_rows = sc.shape[1]
    row_spec = pl.BlockSpec((tm, d), lambda i: (i, 0))
    mod_spec = pl.BlockSpec((1, mod_rows, d), lambda i: (i // tiles_per_mod, 0, 0))
    vec_spec = pl.BlockSpec((1, d), lambda i: (0, 0))
    return pl.pallas_call(
        _mlp_kernel,
        grid=(rows // tm,),
        in_specs=[row_spec, mod_spec, mod_spec, mod_spec, vec_spec, vec_spec,
                  pl.BlockSpec((d, hid), lambda i: (0, 0)),
                  pl.BlockSpec((hid, d), lambda i: (0, 0))],
        out_specs=row_spec,
        out_shape=_sds((rows, d)),
        compiler_params=_cparams(("arbitrary",)),
        name="mlp",
    )(x2, sc, sh, gt, g1, g2, wu, wd)


def _block_diag_tiles(w, per_tile):
    nblk, d, _ = w.shape
    wt = w.reshape(nblk // per_tile, per_tile, d, d)
    eye = jnp.eye(per_tile, dtype=w.dtype)
    bd = wt[:, :, :, None, :] * eye[None, :, None, :, None]
    return bd.reshape(nblk // per_tile, per_tile * d, per_tile * d)


def _layer_weights(l, w_in, w_branch_ssd, w_branch_moba, w_branch_lru, w_out, w_up, w_down,
                   lru_wa, lru_wx):
    o = [0]
    for s in (SSD_D_INNER, SSD_CONV_DIM, SSD_N_HEADS, MOBA_WIDTH, MOBA_WIDTH, MOBA_WIDTH,
              LRU_WIDTH, LRU_WIDTH, N_BRANCH * D_MODEL):
        o.append(o[-1] + s)
    w = w_in[l]
    seg = lambda n: w[:, o[n]:o[n + 1]]
    w_all = jnp.concatenate([seg(1), seg(0), seg(3), seg(6), seg(7), seg(8), seg(4), seg(5)],
                            axis=1).astype(BF16)
    wdt = jnp.pad(seg(2), ((0, 0), (0, DT_PAD - SSD_N_HEADS))).astype(BF16)
    per_tile = 256 // LRU_BLOCK
    return dict(
        w_all=w_all, wdt=wdt,
        ws=w_branch_ssd[l].astype(BF16), wm=w_branch_moba[l].astype(BF16),
        wl=w_branch_lru[l].astype(BF16), wo=w_out[l].astype(BF16),
        wu=w_up[l].astype(BF16), wd=w_down[l].astype(BF16),
        wa_bd=_block_diag_tiles(lru_wa[l], per_tile).astype(BF16),
        wx_bd=_block_diag_tiles(lru_wx[l], per_tile).astype(BF16),
    )


def _row(v):
    return v.reshape(1, -1)


def _pad_lanes(v, width):
    return jnp.pad(v, (0, width - v.shape[0])).reshape(1, width)


def _halo(buf):
    return jnp.pad(buf, ((0, 0), (CONV_HALO - buf.shape[1], 0), (0, 0)))


def _group_layer(x3, mods, wts, prm, ssd_buf, ssd_h, lru_buf, lru_h, moba_fn, *, t_valid, tm_proj,
                 tm_dense, ssd_rows, lru_tm, rows_per_mod):
    bsz, t_pad, d = x3.shape
    rows = bsz * t_pad
    x2 = x3.reshape(rows, d)
    tiles_per_mod = rows_per_mod // tm_dense
    sh1, sc1, gt1, sh2, sc2, gt2 = mods

    p2, dt2, k2, v2 = _inproj_call(x2, sc1, sh1, _row(prm['g_pre_mix']), wts['w_all'], wts['wdt'],
                                   tm=tm_proj, tiles_per_mod=rows_per_mod // tm_proj)
    p3 = p2.reshape(bsz, t_pad, P_WIDTH)
    dt3 = dt2.reshape(bsz, t_pad, DT_PAD)
    k3 = k2.reshape(bsz, t_pad, MOBA_WIDTH)
    v3 = v2.reshape(bsz, t_pad, MOBA_WIDTH)

    y_ssd, ssd_h_new = _ssd_call(
        p3, dt3, _halo(ssd_buf), ssd_h.reshape(bsz, SSD_N_HEADS * SSD_HEAD_DIM, SSD_D_STATE),
        prm['ssd_conv_w'], _row(prm['ssd_conv_b']), _pad_lanes(prm['ssd_dt_bias'], DT_PAD),
        _pad_lanes(prm['ssd_a_log'], DT_PAD), _row(jnp.repeat(prm['ssd_d'], SSD_HEAD_DIM)),
        _row(prm['ssd_norm_g']), rows_in=ssd_rows, t_valid=t_valid)

    y_lru, lru_h_new = _lru_call(
        p3, _halo(lru_buf), lru_h.reshape(bsz, 1, LRU_WIDTH), prm['lru_conv_w'],
        _row(prm['lru_conv_b']), wts['wa_bd'], wts['wx_bd'], _row(prm['lru_ba']),
        _row(prm['lru_bx']), _row(prm['lru_lambda']), tm=lru_tm, t_valid=t_valid)

    y_moba = moba_fn(p3, k3, v3)

    x1 = _merge_call(y_ssd.reshape(rows, d), y_moba.reshape(rows, d), y_lru.reshape(rows, d), p2, x2,
                     gt1, _row(prm['g_post_mix']), wts['ws'], wts['wm'], wts['wl'], wts['wo'],
                     tm=tm_dense, tiles_per_mod=tiles_per_mod)
    x_out = _mlp_call(x1, sc2, sh2, gt2, _row(prm['g_pre_mlp']), _row(prm['g_post_mlp']),
                      wts['wu'], wts['wd'], tm=tm_dense, tiles_per_mod=tiles_per_mod)

    lo = t_valid - (SSD_CONV - 1)
    k_new = k3[:, :t_valid].reshape(bsz, t_valid, MOBA_N_HEADS, MOBA_HEAD_DIM)
    v_new = v3[:, :t_valid].reshape(bsz, t_valid, MOBA_N_HEADS, MOBA_HEAD_DIM)
    ssd_buf_new = p3[:, lo:t_valid, COL_XBC:COL_XBC + SSD_CONV_DIM]
    lru_buf_new = p3[:, lo:t_valid, COL_XR:COL_XR + LRU_WIDTH]
    return (x_out.reshape(bsz, t_pad, d), k_new, v_new, ssd_buf_new,
            ssd_h_new.reshape(bsz, SSD_N_HEADS, SSD_HEAD_DIM, SSD_D_STATE), lru_buf_new,
            lru_h_new.reshape(bsz, LRU_WIDTH))


def kernel(x_prompt, x_sample, c_prompt, c_sample, cache_k, cache_v, page_table, state_ssm, state_ssm_conv, state_lru, state_lru_conv, w_mod, b_mod, g_pre_mix, g_post_mix, g_pre_mlp, g_post_mlp, w_in, ssd_conv_w, ssd_conv_b, ssd_dt_bias, ssd_a_log, ssd_d, ssd_norm_g, lru_conv_w, lru_conv_b, lru_wa, lru_ba, lru_wx, lru_bx, lru_lambda, w_branch_ssd, w_branch_moba, w_branch_lru, w_out, w_up, w_down):
    bp, tp, d = x_prompt.shape
    bs, ts, _ = x_sample.shape
    depth = w_in.shape[0]
    n_pool = cache_k.shape[1]
    assert ts <= SAMPLE_T_PAD and ts >= SSD_CONV - 1
    assert (page_table.shape[1] * PAGE_SIZE) % MOBA_BLOCK == 0
    assert tp % SSD_CHUNK == 0 and tp % MOBA_BLOCK == 0

    tm_p = min(1024, tp)
    tm_dense_p = min(256, tp)
    lru_tm_p = min(256, tp)

    n_c = bp + bs
    n_c_pad = -(-n_c // SUBLANES) * SUBLANES
    c_all = jnp.pad(jnp.concatenate([c_prompt, c_sample], axis=0), ((0, n_c_pad - n_c), (0, 0)))
    mod_all = _mod_call(c_all, w_mod, b_mod)

    xs_pad = jnp.pad(x_sample, ((0, 0), (0, SAMPLE_T_PAD - ts), (0, 0)))

    zero_ssd_buf = jnp.zeros((bp, SSD_CONV - 1, SSD_CONV_DIM), F32)
    zero_ssd_h = jnp.zeros((bp, SSD_N_HEADS, SSD_HEAD_DIM, SSD_D_STATE), F32)
    zero_lru_buf = jnp.zeros((bp, LRU_CONV - 1, LRU_WIDTH), F32)
    zero_lru_h = jnp.zeros((bp, LRU_WIDTH), F32)

    outs_p = [[] for _ in range(6)]
    outs_s = [[] for _ in range(6)]
    xp, xs = x_prompt, xs_pad
    for l in range(depth):
        prm = {
            'g_pre_mix': g_pre_mix[l], 'g_post_mix': g_post_mix[l], 'g_pre_mlp': g_pre_mlp[l],
            'g_post_mlp': g_post_mlp[l], 'ssd_conv_w': ssd_conv_w[l], 'ssd_conv_b': ssd_conv_b[l],
            'ssd_dt_bias': ssd_dt_bias[l], 'ssd_a_log': ssd_a_log[l], 'ssd_d': ssd_d[l],
            'ssd_norm_g': ssd_norm_g[l], 'lru_conv_w': lru_conv_w[l], 'lru_conv_b': lru_conv_b[l],
            'lru_ba': lru_ba[l].reshape(-1), 'lru_bx': lru_bx[l].reshape(-1),
            'lru_lambda': lru_lambda[l],
        }
        wts = _layer_weights(l, w_in, w_branch_ssd, w_branch_moba, w_branch_lru, w_out, w_up,
                             w_down, lru_wa, lru_wx)
        mod_l = mod_all[l]
        mods_p = [mod_l[:bp, n * d:(n + 1) * d].reshape(bp, 1, d) for n in range(6)]
        mods_s = [jnp.repeat(mod_l[bp:bp + bs, n * d:(n + 1) * d], SAMPLE_T_PAD, axis=0)
                  .reshape(1, bs * SAMPLE_T_PAD, d) for n in range(6)]

        res_p = _group_layer(xp, mods_p, wts, prm, zero_ssd_buf, zero_ssd_h, zero_lru_buf, zero_lru_h,
                             _moba_prompt_call, t_valid=tp, tm_proj=tm_p, tm_dense=tm_dense_p,
                             ssd_rows=SSD_CHUNK, lru_tm=lru_tm_p, rows_per_mod=tp)
        xp = res_p[0]
        for n in range(6):
            outs_p[n].append(res_p[n + 1])

        def moba_sample(p3, k3, v3, l=l):
            ksum = _ksum_call(page_table, cache_k, l)
            sel = _select_call(p3, ksum)
            sel = sel[:, :ts, :MOBA_N_HEADS * 4].reshape(bs, ts, MOBA_N_HEADS, 4)[..., :MOBA_TOPK]
            return _moba_sample_call(page_table, sel, p3, k3, v3, cache_k, cache_v, l, t_valid=ts)

        res_s = _group_layer(xs, mods_s, wts, prm, state_ssm_conv[l], state_ssm[l], state_lru_conv[l],
                             state_lru[l], moba_sample, t_valid=ts, tm_proj=bs * SAMPLE_T_PAD,
                             tm_dense=bs * SAMPLE_T_PAD, ssd_rows=SAMPLE_T_PAD, lru_tm=SAMPLE_T_PAD,
                             rows_per_mod=bs * SAMPLE_T_PAD)
        xs = res_s[0]
        for n in range(6):
            outs_s[n].append(res_s[n + 1])

    st = lambda lst: jnp.stack(lst)
    kp, vp, ssmc_p, ssm_p, lruc_p, lru_p = outs_p
    ks, vs, ssmc_s, ssm_s, lruc_s, lru_s = outs_s
    return (xp, xs[:, :ts],
            st(kp), st(vp), st(ks), st(vs),
            st(ssm_p), st(ssmc_p), st(lru_p), st(lruc_p),
            st(ssm_s), st(ssmc_s), st(lru_s), st(lruc_s))
```

```python
import functools
import math

import jax
import jax.numpy as jnp
from jax import lax
from jax.experimental import pallas as pl
from jax.experimental.pallas import tpu as pltpu

F32 = jnp.float32
BF16 = jnp.bfloat16
HIGHEST = lax.Precision.HIGHEST

D_MODEL = 1024
PAGE_SIZE = 128
SSD_D_INNER = D_MODEL
SSD_HEAD_DIM = 64
SSD_N_HEADS = SSD_D_INNER // SSD_HEAD_DIM
SSD_N_GROUPS = 4
SSD_D_STATE = 128
SSD_CONV = 4
SSD_CHUNK = 128
SSD_CONV_DIM = SSD_D_INNER + 2 * SSD_N_GROUPS * SSD_D_STATE
MOBA_N_HEADS = 8
MOBA_HEAD_DIM = 128
MOBA_WIDTH = MOBA_N_HEADS * MOBA_HEAD_DIM
MOBA_BLOCK = 256
MOBA_TOPK = 3
LRU_WIDTH = D_MODEL
LRU_N_BLOCKS = 16
LRU_BLOCK = LRU_WIDTH // LRU_N_BLOCKS
LRU_CONV = 4
LRU_C = 8.0
N_BRANCH = 3
MLP_HIDDEN = 4 * D_MODEL
RMS_EPS = 1e-6

SUBLANES = 8
LANES = 128
VMEM_LIMIT_BYTES = 56 * 1024 * 1024

SAMPLE_T_PAD = SUBLANES
CONV_HALO = SUBLANES
NEG_BIG = -1e30
LOG2E = math.log2(math.e)

COL_XBC = 0
COL_Z = COL_XBC + SSD_CONV_DIM
COL_Q = COL_Z + SSD_D_INNER
COL_XR = COL_Q + MOBA_WIDTH
COL_GR = COL_XR + LRU_WIDTH
COL_GATE = COL_GR + LRU_WIDTH
P_WIDTH = COL_GATE + N_BRANCH * D_MODEL
PROJ_TN = 1024
P_TILES = P_WIDTH // PROJ_TN
DT_PAD = LANES


def _cparams(sem):
    return pltpu.CompilerParams(dimension_semantics=sem, vmem_limit_bytes=VMEM_LIMIT_BYTES)


def _sds(shape, dtype=F32):
    return jax.ShapeDtypeStruct(shape, dtype)


def _silu(x):
    return x * jax.nn.sigmoid(x)


def _softplus(x):
    return jnp.maximum(x, 0.0) + jnp.log1p(jnp.exp(-jnp.abs(x)))


def _rms(x):
    return x * lax.rsqrt(jnp.mean(x * x, axis=-1, keepdims=True) + RMS_EPS)


def _dot(a, b):
    return jnp.dot(a, b, preferred_element_type=F32)


def _dot_nt(a, b, precision=None):
    return lax.dot_general(a, b, (((1,), (1,)), ((), ())), precision=precision,
                           preferred_element_type=F32)


def _mod_kernel(c_ref, w_ref, b_ref, o_ref):
    s = _silu(c_ref[...]).astype(BF16)
    o_ref[0] = _dot(s, w_ref[0].astype(BF16)) + b_ref[0]


def _mod_call(c_all, w_mod, b_mod):
    depth, d, n = w_mod.shape
    rows = c_all.shape[0]
    tn = 1536
    return pl.pallas_call(
        _mod_kernel,
        grid=(depth, n // tn),
        in_specs=[pl.BlockSpec((rows, d), lambda l, j: (0, 0)),
                  pl.BlockSpec((1, d, tn), lambda l, j: (l, 0, j)),
                  pl.BlockSpec((1, 1, tn), lambda l, j: (l, 0, j))],
        out_specs=pl.BlockSpec((1, rows, tn), lambda l, j: (l, 0, j)),
        out_shape=_sds((depth, rows, n)),
        compiler_params=_cparams(("arbitrary", "arbitrary")),
        name="mod",
    )(c_all, w_mod, b_mod.reshape(depth, 1, n))


def _inproj_kernel(x_ref, sc_ref, sh_ref, g_ref, w_ref, wdt_ref, p_ref, dt_ref, k_ref, v_ref, h_scr):
    j = pl.program_id(1)

    @pl.when(j == 0)
    def _():
        y = _rms(x_ref[...]) * g_ref[...]
        hb = (y * (1.0 + sc_ref[0]) + sh_ref[0]).astype(BF16)
        h_scr[...] = hb
        dt_ref[...] = _dot(hb, wdt_ref[...])

    @pl.when(j < P_TILES)
    def _():
        p_ref[...] = _dot(h_scr[...], w_ref[...])

    @pl.when(j == P_TILES)
    def _():
        k_ref[...] = _dot(h_scr[...], w_ref[...])

    @pl.when(j == P_TILES + 1)
    def _():
        v_ref[...] = _dot(h_scr[...], w_ref[...])


def _inproj_call(x2, sc, sh, g, w_all, wdt, *, tm, tiles_per_mod):
    rows, d = x2.shape
    assert PROJ_TN == MOBA_WIDTH and w_all.shape[1] == P_WIDTH + 2 * MOBA_WIDTH
    mod_rows = sc.shape[1]
    mod_spec = pl.BlockSpec((1, mod_rows, d), lambda i, j: (i // tiles_per_mod, 0, 0))
    kv_spec = pl.BlockSpec((tm, MOBA_WIDTH), lambda i, j: (i, 0))
    return pl.pallas_call(
        _inproj_kernel,
        grid=(rows // tm, P_TILES + 2),
        in_specs=[pl.BlockSpec((tm, d), lambda i, j: (i, 0)),
                  mod_spec, mod_spec,
                  pl.BlockSpec((1, d), lambda i, j: (0, 0)),
                  pl.BlockSpec((d, PROJ_TN), lambda i, j: (0, j)),
                  pl.BlockSpec((d, DT_PAD), lambda i, j: (0, 0))],
        out_specs=[pl.BlockSpec((tm, PROJ_TN), lambda i, j: (i, jnp.minimum(j, P_TILES - 1))),
                   pl.BlockSpec((tm, DT_PAD), lambda i, j: (i, 0)),
                   kv_spec, kv_spec],
        out_shape=[_sds((rows, P_WIDTH)), _sds((rows, DT_PAD)),
                   _sds((rows, MOBA_WIDTH)), _sds((rows, MOBA_WIDTH))],
        scratch_shapes=[pltpu.VMEM((tm, d), BF16)],
        compiler_params=_cparams(("arbitrary", "arbitrary")),
        name="inproj",
    )(x2, sc, sh, g, w_all, wdt)


def _conv_tile(prev_scr, x, buf_ref, w_ref, b_ref, tile_idx):
    rows, width = x.shape

    @pl.when(tile_idx == 0)
    def _():
        prev_scr[...] = buf_ref[0]

    prev = prev_scr[...]
    prev_scr[...] = x[rows - CONV_HALO:, :]
    n_groups = rows // CONV_HALO
    x3 = jnp.concatenate([prev, x], axis=0).reshape(n_groups + 1, CONV_HALO, width)
    row = lax.broadcasted_iota(jnp.int32, (1, CONV_HALO, width), 1)
    acc = b_ref[...] + w_ref[SSD_CONV - 1:SSD_CONV, :] * x
    for k in range(1, SSD_CONV):
        r3 = pltpu.roll(x3, k, 1)
        shifted = jnp.where(row >= k, r3[1:], r3[:-1]).reshape(rows, width)
        acc = acc + w_ref[SSD_CONV - 1 - k:SSD_CONV - k, :] * shifted
    return acc


def _ssd_kernel(xbc_ref, z_ref, dt_ref, buf_ref, h0_ref, cw_ref, cb_ref, dtb_ref, alog_ref,
                dsk_ref, ng_ref, y_ref, h_ref, xp_scr, dt_scr, *, rows_in, t_valid):
    L = SSD_CHUNK
    N = SSD_D_STATE
    P2 = 2 * SSD_HEAD_DIM
    c = pl.program_id(1)

    @pl.when(c == 0)
    def _():
        h_ref[0] = h0_ref[0]

    xc = _silu(_conv_tile(xp_scr, xbc_ref[0], buf_ref, cw_ref, cb_ref, c))
    if rows_in < L:
        xc = jnp.concatenate([xc, jnp.zeros((L - rows_in, SSD_CONV_DIM), F32)], axis=0)
    xs = xc[:, :SSD_D_INNER]
    bm = xc[:, SSD_D_INNER:SSD_D_INNER + SSD_N_GROUPS * N]
    cm = xc[:, SSD_D_INNER + SSD_N_GROUPS * N:]

    if rows_in < L:
        dt_scr[...] = jnp.zeros((L, LANES), F32)
    dt_scr[0:rows_in, :] = dt_ref[0]
    lane = lax.broadcasted_iota(jnp.int32, (L, LANES), 1)
    row = lax.broadcasted_iota(jnp.int32, (L, LANES), 0)
    dtv = _softplus(dt_scr[...] + dtb_ref[...])
    a = jnp.where(lane < SSD_N_HEADS, -jnp.exp(alog_ref[...]), 0.0)
    dtv = jnp.where((row + c * L < t_valid) & (lane < SSD_N_HEADS), dtv, 0.0)
    dta = dtv * a
    ri = lax.broadcasted_iota(jnp.int32, (L, L), 0)
    ci = lax.broadcasted_iota(jnp.int32, (L, L), 1)
    causal = ri >= ci
    cum = jnp.dot(causal.astype(F32), dta, precision=HIGHEST, preferred_element_type=F32)
    cum_t = cum.T
    dtv_t = dtv.T
    ecum = jnp.exp(cum)
    cum_last = cum[L - 1:L, :]
    wgt = jnp.exp(cum_last - cum) * dtv
    cdec_t = jnp.exp(cum_t[:, L - 1:L])
    left = lax.broadcasted_iota(jnp.int32, (L, P2), 1) < SSD_HEAD_DIM
    left_rows = lax.broadcasted_iota(jnp.int32, (P2, N), 0) < SSD_HEAD_DIM

    y_tiles = []
    for j in range(SSD_N_HEADS // 2):
        g = j // (SSD_N_HEADS // SSD_N_GROUPS // 2)
        h_a, h_b = 2 * j, 2 * j + 1
        bm_g = bm[:, g * N:(g + 1) * N]
        cm_g = cm[:, g * N:(g + 1) * N].astype(BF16)
        cb = _dot_nt(cm_g, bm_g.astype(BF16))
        xs_p = xs[:, j * P2:(j + 1) * P2]
        m_parts = []
        for hh in (h_a, h_b):
            seg = cum[:, hh:hh + 1] - cum_t[hh:hh + 1, :]
            dec = jnp.exp(jnp.where(causal, seg, -jnp.inf))
            m_parts.append((cb * dec * dtv_t[hh:hh + 1, :]).astype(BF16))
        m_pair = jnp.concatenate(m_parts, axis=1)
        x_bd = jnp.concatenate([jnp.where(left, xs_p, 0.0), jnp.where(left, 0.0, xs_p)],
                               axis=0).astype(BF16)
        y_diag = _dot(m_pair, x_bd)
        h_prev = h_ref[0, j * P2:(j + 1) * P2, :]
        y_off = _dot_nt(cm_g, h_prev.astype(BF16))
        e_sel = jnp.where(left, ecum[:, h_a:h_a + 1], ecum[:, h_b:h_b + 1])
        y_tiles.append(y_diag + y_off * e_sel + dsk_ref[:, j * P2:(j + 1) * P2] * xs_p)
        w_sel = jnp.where(left, wgt[:, h_a:h_a + 1], wgt[:, h_b:h_b + 1])
        xw_t = (xs_p * w_sel).T.astype(BF16)
        st = _dot(xw_t, bm_g.astype(BF16))
        d_sel = jnp.where(left_rows, cdec_t[h_a:h_a + 1, :], cdec_t[h_b:h_b + 1, :])
        h_ref[0, j * P2:(j + 1) * P2, :] = h_prev * d_sel + st

    gw = SSD_D_INNER // SSD_N_GROUPS
    tiles_per_group = gw // P2
    zs = _silu(z_ref[0])
    for g in range(SSD_N_GROUPS):
        yg = jnp.concatenate(y_tiles[g * tiles_per_group:(g + 1) * tiles_per_group], axis=1)
        yg = yg[0:rows_in, :] * zs[:, g * gw:(g + 1) * gw]
        y_ref[0, :, g * gw:(g + 1) * gw] = (_rms(yg) * ng_ref[:, g * gw:(g + 1) * gw]).astype(BF16)


def _ssd_call(p3, dt3, buf8, h0, cw, cb, dtb, alog, dsk, ng, *, rows_in, t_valid):
    bsz, t_pad, _ = p3.shape
    nc = t_pad // rows_in
    L = SSD_CHUNK
    hp = SSD_N_HEADS * SSD_HEAD_DIM
    const2 = lambda b, c: (0, 0)
    kern = functools.partial(_ssd_kernel, rows_in=rows_in, t_valid=t_valid)
    return pl.pallas_call(
        kern,
        grid=(bsz, nc),
        in_specs=[pl.BlockSpec((1, rows_in, SSD_CONV_DIM), lambda b, c: (b, c, COL_XBC // SSD_CONV_DIM)),
                  pl.BlockSpec((1, rows_in, SSD_D_INNER), lambda b, c: (b, c, COL_Z // SSD_D_INNER)),
                  pl.BlockSpec((1, rows_in, DT_PAD), lambda b, c: (b, c, 0)),
                  pl.BlockSpec((1, CONV_HALO, SSD_CONV_DIM), lambda b, c: (b, 0, 0)),
                  pl.BlockSpec((1, hp, SSD_D_STATE), lambda b, c: (b, 0, 0)),
                  pl.BlockSpec((SSD_CONV, SSD_CONV_DIM), const2),
                  pl.BlockSpec((1, SSD_CONV_DIM), const2),
                  pl.BlockSpec((1, DT_PAD), const2),
                  pl.BlockSpec((1, DT_PAD), const2),
                  pl.BlockSpec((1, SSD_D_INNER), const2),
                  pl.BlockSpec((1, SSD_D_INNER), const2)],
        out_specs=[pl.BlockSpec((1, rows_in, SSD_D_INNER), lambda b, c: (b, c, 0)),
                   pl.BlockSpec((1, hp, SSD_D_STATE), lambda b, c: (b, 0, 0))],
        out_shape=[_sds((bsz, t_pad, SSD_D_INNER), BF16), _sds((bsz, hp, SSD_D_STATE))],
        scratch_shapes=[pltpu.VMEM((CONV_HALO, SSD_CONV_DIM), F32),
                        pltpu.VMEM((L, DT_PAD), F32)],
        compiler_params=_cparams(("arbitrary", "arbitrary")),
        name="ssd",
    )(p3, p3, dt3, buf8, h0, cw, cb, dtb, alog, dsk, ng)


def _lru_kernel(xr_ref, gr_ref, buf_ref, h0_ref, cw_ref, cb_ref, wa_ref, wx_ref, ba_ref, bx_ref,
                lam_ref, y_ref, h_ref, xp_scr, *, tm, t_valid, t_pad):
    i = pl.program_id(1)

    @pl.when(i == 0)
    def _():
        h_ref[0] = h0_ref[0]

    xc = _conv_tile(xp_scr, xr_ref[0], buf_ref, cw_ref, cb_ref, i)
    bw = wa_ref.shape[1]
    r_parts, i_parts = [], []
    for j in range(LRU_WIDTH // bw):
        xj = xc[:, j * bw:(j + 1) * bw].astype(BF16)
        r_parts.append(_dot(xj, wa_ref[j]))
        i_parts.append(_dot(xj, wx_ref[j]))
    r = jax.nn.sigmoid(jnp.concatenate(r_parts, axis=1) + ba_ref[...])
    ig = jax.nn.sigmoid(jnp.concatenate(i_parts, axis=1) + bx_ref[...])
    log_a = (-LRU_C) * r * _softplus(-lam_ref[...])
    a_full = jnp.exp(log_a)
    b_full = jnp.sqrt(-jnp.tanh(log_a) * (a_full * a_full + 1.0)) * (ig * xc)
    gate = jax.nn.gelu(gr_ref[0], approximate=True)

    n_groups = tm // SUBLANES
    row = lax.broadcasted_iota(jnp.int32, (tm, LANES), 0)
    row_in_group = lax.broadcasted_iota(jnp.int32, (1, SUBLANES, LANES), 1)
    for s in range(LRU_WIDTH // LANES):
        sl = slice(s * LANES, (s + 1) * LANES)
        a, b = a_full[:, sl], b_full[:, sl]
        if t_valid < t_pad:
            valid = row + i * tm < t_valid
            a = jnp.where(valid, a, 1.0)
            b = jnp.where(valid, b, 0.0)
        a = a.reshape(n_groups, SUBLANES, LANES)
        b = b.reshape(n_groups, SUBLANES, LANES)
        k = 1
        while k < SUBLANES:
            keep = row_in_group >= k
            b = b + a * jnp.where(keep, pltpu.roll(b, k, 1), 0.0)
            a = a * jnp.where(keep, pltpu.roll(a, k, 1), 1.0)
            k *= 2
        carry = h_ref[0, :, sl]
        groups = []
        for g in range(n_groups):
            hg = b[g] + a[g] * carry
            groups.append(hg)
            carry = hg[SUBLANES - 1:SUBLANES, :]
        h = groups[0] if n_groups == 1 else jnp.concatenate(groups, axis=0)
        y_ref[0, :, sl] = (h * gate[:, sl]).astype(BF16)
        h_ref[0, :, sl] = carry


def _lru_call(p3, buf8, h0, cw, cb, wa_bd, wx_bd, ba, bx, lam, *, tm, t_valid):
    bsz, t_pad, _ = p3.shape
    W = LRU_WIDTH
    nb, bw, _ = wa_bd.shape
    const2 = lambda b, i: (0, 0)
    const3 = lambda b, i: (0, 0, 0)
    kern = functools.partial(_lru_kernel, tm=tm, t_valid=t_valid, t_pad=t_pad)
    return pl.pallas_call(
        kern,
        grid=(bsz, t_pad // tm),
        in_specs=[pl.BlockSpec((1, tm, W), lambda b, i: (b, i, COL_XR // W)),
                  pl.BlockSpec((1, tm, W), lambda b, i: (b, i, COL_GR // W)),
                  pl.BlockSpec((1, CONV_HALO, W), lambda b, i: (b, 0, 0)),
                  pl.BlockSpec((1, 1, W), lambda b, i: (b, 0, 0)),
                  pl.BlockSpec((LRU_CONV, W), const2),
                  pl.BlockSpec((1, W), const2),
                  pl.BlockSpec((nb, bw, bw), const3),
                  pl.BlockSpec((nb, bw, bw), const3),
                  pl.BlockSpec((1, W), const2),
                  pl.BlockSpec((1, W), const2),
                  pl.BlockSpec((1, W), const2)],
        out_specs=[pl.BlockSpec((1, tm, W), lambda b, i: (b, i, 0)),
                   pl.BlockSpec((1, 1, W), lambda b, i: (b, 0, 0))],
        out_shape=[_sds((bsz, t_pad, W), BF16), _sds((bsz, 1, W))],
        scratch_shapes=[pltpu.VMEM((CONV_HALO, W), F32)],
        compiler_params=_cparams(("arbitrary", "arbitrary")),
        name="lru",
    )(p3, p3, buf8, h0, cw, cb, wa_bd, wx_bd, ba, bx, lam)


def _topk_mask(gate, n_valid, col):
    nb = gate.shape[1]
    past = col < n_valid
    gm = jnp.where(past, gate, -jnp.inf)
    rank = jnp.zeros(gate.shape, F32)
    for j in range(nb):
        gj = gm[:, j:j + 1]
        ahead = (gj > gm) | ((gj == gm) & (col > j))
        rank = rank + jnp.where(ahead, 1.0, 0.0)
    return (rank < MOBA_TOPK) & past, rank


PAGES_PER_BLOCK = MOBA_BLOCK // PAGE_SIZE


def _ksum_stream(pt_ref, ck_hbm, ks_ref, kbuf, sem, step, n_steps, *, layer, pps, n_pages):
    def copies(st, slot, lookup):
        out = []
        for u in range(pps):
            if lookup:
                p = st * pps + u
                page = pt_ref[lax.div(p, n_pages), lax.rem(p, n_pages)]
            else:
                page = 0
            out.append(pltpu.make_async_copy(ck_hbm.at[layer, page], kbuf.at[slot, u], sem.at[slot]))
        return out

    slot = lax.rem(step, 2)

    @pl.when(step == 0)
    def _():
        for c in copies(step, slot, True):
            c.start()

    @pl.when(step + 1 < n_steps)
    def _():
        for c in copies(step + 1, 1 - slot, True):
            c.start()

    for c in copies(step, slot, False):
        c.wait()

    blk0 = lax.div(lax.rem(step * pps, n_pages), PAGES_PER_BLOCK)

    @pl.when(blk0 == 0)
    def _():
        ks_ref[...] = jnp.zeros(ks_ref.shape, F32)

    sums = []
    for r in range(pps // PAGES_PER_BLOCK):
        acc = jnp.sum(kbuf[slot, r * PAGES_PER_BLOCK], axis=0)
        for u in range(1, PAGES_PER_BLOCK):
            acc = acc + jnp.sum(kbuf[slot, r * PAGES_PER_BLOCK + u], axis=0)
        sums.append(acc)
    row = lax.broadcasted_iota(jnp.int32, ks_ref.shape[2:], 0)
    for h in range(MOBA_N_HEADS):
        cur = ks_ref[0, h]
        for r, acc in enumerate(sums):
            cur = jnp.where(row == blk0 + r, acc[h:h + 1, :], cur)
        ks_ref[0, h] = cur


def _moba_prompt_kernel(pt_ref, q_ref, k_ref, v_ref, ck_hbm, o_ref, ks_ref, kmean_scr, kg_scr, vtg_scr,
                        sel_scr, kbuf, sem, *, nb, grp, nh, layer, pps, n_pages):
    i = pl.program_id(2)
    BS = MOBA_BLOCK
    GB = grp * BS
    dh = MOBA_HEAD_DIM
    scale = MOBA_HEAD_DIM ** -0.5

    step = (pl.program_id(0) * pl.num_programs(1) + pl.program_id(1)) * pl.num_programs(2) + i
    n_steps = pl.num_programs(0) * pl.num_programs(1) * pl.num_programs(2)
    _ksum_stream(pt_ref, ck_hbm, ks_ref, kbuf, sem, step, n_steps, layer=layer, pps=pps,
                 n_pages=n_pages)

    @pl.when(i == 0)
    def _():
        for hh in range(nh):
            hs = slice(hh * dh, (hh + 1) * dh)
            for j in range(nb):
                kmean_scr[hh, j:j + 1, :] = jnp.mean(k_ref[0, j * BS:(j + 1) * BS, hs], axis=0,
                                                     keepdims=True)
            for jj in range(nb // grp):
                kg_scr[hh, jj] = k_ref[0, jj * GB:(jj + 1) * GB, hs].astype(BF16)
                for u in range(grp):
                    r0 = jj * GB + u * BS
                    vtg_scr[hh, jj, :, u * BS:(u + 1) * BS] = v_ref[0, r0:r0 + BS, hs].T.astype(BF16)

    qb_t = []
    for hh in range(nh):
        q_t = q_ref[0, :, hh * dh:(hh + 1) * dh].T
        qb_t.append(q_t.astype(BF16))
        gate = jnp.dot(kmean_scr[hh], q_t, precision=HIGHEST, preferred_element_type=F32)
        blk = lax.broadcasted_iota(jnp.int32, gate.shape, 0)
        past = blk < i
        gm = jnp.where(past, gate, -jnp.inf)
        rank = jnp.zeros(gate.shape, F32)
        for j in range(nb):
            gj = gm[j:j + 1, :]
            ahead = (gj > gm) | ((gj == gm) & (blk > j))
            rank = rank + jnp.where(ahead, 1.0, 0.0)
        sel = jnp.where((rank < MOBA_TOPK) & past, 0.0, -jnp.inf)
        for j in range(nb):
            sel_scr[hh, j] = jnp.broadcast_to(sel[j:j + 1, :], (SUBLANES, BS))

    def block_bias(hh, j):
        rows = sel_scr[hh, j]
        return jnp.broadcast_to(rows[None], (BS // SUBLANES, SUBLANES, BS)).reshape(BS, BS)

    sub = min(MOBA_SUB, grp)
    n_parts = grp // sub
    SB = sub * BS

    def scores(hh, jj, part):
        return _dot(kg_scr[hh, jj, part * SB:(part + 1) * SB, :], qb_t[hh])

    def update(carry, hh, jj, part, s_raw, bias):
        m, l, acc = carry
        s = s_raw * (scale * LOG2E) + bias
        m_new = jnp.maximum(m, jnp.max(s, axis=0, keepdims=True))
        alpha = jnp.exp2(m - m_new)
        p = jnp.exp2(s - m_new)
        l_new = alpha * l + jnp.sum(p, axis=0, keepdims=True)
        v_t = vtg_scr[hh, jj, :, part * SB:(part + 1) * SB]
        acc_new = alpha * acc + _dot(v_t, p.astype(BF16))
        return m_new, l_new, acc_new

    def run_parts(jj, carries, bias_fn, parts):
        s_raw = {part: [scores(hh, jj, part) for hh in range(nh)] for part in parts}
        carries = list(carries)
        for part in parts:
            for hh in range(nh):
                bias = jnp.concatenate(
                    [bias_fn(hh, jj * grp + part * sub + u) for u in range(sub)], axis=0)
                carries[hh] = update(carries[hh], hh, jj, part, s_raw[part][hh], bias)
        return tuple(carries)

    def body(jj, carries):
        return run_parts(jj, carries, block_bias, range(n_parts))

    init = tuple((jnp.full((1, BS), NEG_BIG, F32), jnp.zeros((1, BS), F32), jnp.zeros((dh, BS), F32))
                 for _ in range(nh))
    last = i // grp
    carries = lax.fori_loop(0, last, body, init)

    key = lax.broadcasted_iota(jnp.int32, (BS, BS), 0)
    qry = lax.broadcasted_iota(jnp.int32, (BS, BS), 1)
    causal_bias = jnp.where(key <= qry, 0.0, -jnp.inf)

    def last_bias(hh, j):
        return jnp.where(j == i, causal_bias, block_bias(hh, j))

    carries = run_parts(last, carries, last_bias, [0])
    for part in range(1, n_parts):
        carries = lax.cond(last * grp + part * sub <= i,
                           lambda c, part=part: run_parts(last, c, last_bias, [part]),
                           lambda c: c, carries)
    for hh in range(nh):
        m, l, acc = carries[hh]
        o_ref[0, :, hh * dh:(hh + 1) * dh] = (acc / l).T.astype(BF16)


MOBA_GROUP = 4
MOBA_SUB = 2
MOBA_HEADS_PER_STEP = 2


def _moba_prompt_call(p3, k3, v3, page_table, cache_k, layer):
    bsz, t, _ = p3.shape
    assert t % MOBA_BLOCK == 0 and MOBA_N_HEADS % MOBA_HEADS_PER_STEP == 0
    nb = t // MOBA_BLOCK
    grp = math.gcd(MOBA_GROUP, nb)
    nh = MOBA_HEADS_PER_STEP
    dh = MOBA_HEAD_DIM
    hw = nh * dh
    n_hg = MOBA_N_HEADS // nh
    bs, n_pages = page_table.shape
    n_steps = bsz * n_hg * nb
    pps = (bs * n_pages) // n_steps
    assert pps * n_steps == bs * n_pages and pps % PAGES_PER_BLOCK == 0 and n_pages % pps == 0
    nbs = n_pages // PAGES_PER_BLOCK

    def seq_of(b, h, i):
        return (((b * n_hg + h) * nb + i) * pps) // n_pages

    kern = functools.partial(_moba_prompt_kernel, nb=nb, grp=grp, nh=nh, layer=layer, pps=pps,
                             n_pages=n_pages)
    grid_spec = pltpu.PrefetchScalarGridSpec(
        num_scalar_prefetch=1,
        grid=(bsz, n_hg, nb),
        in_specs=[pl.BlockSpec((1, MOBA_BLOCK, hw), lambda b, h, i, pt: (b, i, COL_Q // hw + h)),
                  pl.BlockSpec((1, t, hw), lambda b, h, i, pt: (b, 0, h)),
                  pl.BlockSpec((1, t, hw), lambda b, h, i, pt: (b, 0, h)),
                  pl.BlockSpec(memory_space=pl.ANY)],
        out_specs=[pl.BlockSpec((1, MOBA_BLOCK, hw), lambda b, h, i, pt: (b, i, h)),
                   pl.BlockSpec((1, MOBA_N_HEADS, nbs, dh), lambda b, h, i, pt: (seq_of(b, h, i), 0, 0, 0))],
        scratch_shapes=[pltpu.VMEM((nh, nb, dh), F32),
                        pltpu.VMEM((nh, nb // grp, grp * MOBA_BLOCK, dh), BF16),
                        pltpu.VMEM((nh, nb // grp, dh, grp * MOBA_BLOCK), BF16),
                        pltpu.VMEM((nh, nb, SUBLANES, MOBA_BLOCK), F32),
                        pltpu.VMEM((2, pps, PAGE_SIZE, MOBA_N_HEADS, dh), F32),
                        pltpu.SemaphoreType.DMA((2,))],
    )
    return pl.pallas_call(
        kern,
        grid_spec=grid_spec,
        out_shape=[_sds((bsz, t, MOBA_WIDTH), BF16), _sds((bs, MOBA_N_HEADS, nbs, dh))],
        compiler_params=_cparams(("arbitrary", "arbitrary", "arbitrary")),
        name="moba_prompt",
    )(page_table, p3, k3, v3, cache_k)


def _select_kernel(q_ref, ks_ref, o_ref, *, nb):
    rows = q_ref.shape[1]
    col = lax.broadcasted_iota(jnp.int32, (rows, nb), 1)
    colf = col.astype(F32)
    lane = lax.broadcasted_iota(jnp.int32, (rows, LANES), 1)
    out = jnp.zeros((rows, LANES), F32)
    for h in range(MOBA_N_HEADS):
        sl = slice(h * MOBA_HEAD_DIM, (h + 1) * MOBA_HEAD_DIM)
        kmean = ks_ref[0, h] * (1.0 / MOBA_BLOCK)
        gate = _dot_nt(q_ref[0, :, sl], kmean, precision=HIGHEST)
        sel, rank = _topk_mask(gate, nb, col)
        for r in range(MOBA_TOPK):
            idx = jnp.sum(jnp.where(sel & (rank == r), colf, 0.0), axis=1, keepdims=True)
            out = jnp.where(lane == h * 4 + r, idx, out)
    o_ref[0] = out.astype(jnp.int32)


def _select_call(p3, ksum):
    bs, rows, _ = p3.shape
    nb = ksum.shape[2]
    assert nb >= MOBA_TOPK
    return pl.pallas_call(
        functools.partial(_select_kernel, nb=nb),
        grid=(bs,),
        in_specs=[pl.BlockSpec((1, rows, MOBA_WIDTH), lambda b: (b, 0, COL_Q // MOBA_WIDTH)),
                  pl.BlockSpec((1, MOBA_N_HEADS, nb, MOBA_HEAD_DIM), lambda b: (b, 0, 0, 0))],
        out_specs=pl.BlockSpec((1, rows, LANES), lambda b: (b, 0, 0)),
        out_shape=_sds((bs, rows, LANES), jnp.int32),
        compiler_params=_cparams(("arbitrary",)),
        name="moba_select",
    )(p3, ksum)


N_SEL_PAGES = MOBA_TOPK * PAGES_PER_BLOCK


def _moba_sample_kernel(pt_ref, sel_ref, q_ref, kn_ref, vn_ref, ck_hbm, cv_hbm, o_ref,
                        kbuf, vbuf, sem, *, layer, t_valid):
    step = pl.program_id(0)
    n_steps = pl.num_programs(0)
    t = step % t_valid
    slot = step % 2
    rows = q_ref.shape[1]
    dh = MOBA_HEAD_DIM
    scale = MOBA_HEAD_DIM ** -0.5

    def gather(st, sl, start):
        b = st // t_valid
        for h in range(MOBA_N_HEADS):
            for r in range(MOBA_TOPK):
                blk = sel_ref[(st * MOBA_N_HEADS + h) * MOBA_TOPK + r] if start else 0
                for half in range(PAGES_PER_BLOCK):
                    page = pt_ref[b, blk * PAGES_PER_BLOCK + half] if start else 0
                    n = r * PAGES_PER_BLOCK + half
                    for src, dst in ((ck_hbm, kbuf), (cv_hbm, vbuf)):
                        cp = pltpu.make_async_copy(src.at[layer, page, :, h, :], dst.at[sl, h, n],
                                                   sem.at[sl])
                        if start:
                            cp.start()
                        else:
                            cp.wait()

    @pl.when(step == 0)
    def _():
        gather(step, slot, True)

    @pl.when(step + 1 < n_steps)
    def _():
        gather(step + 1, 1 - slot, True)

    gather(step, slot, False)

    @pl.when(t == 0)
    def _():
        o_ref[...] = jnp.zeros(o_ref.shape, F32)

    u = lax.broadcasted_iota(jnp.int32, (rows, dh), 0)
    own_ok = (u <= t) & (u < t_valid)
    is_t = u == t
    raw = []
    for h in range(MOBA_N_HEADS):
        sl = slice(h * dh, (h + 1) * dh)
        q_row = jnp.sum(jnp.where(is_t, q_ref[0, :, sl], 0.0), axis=0, keepdims=True)
        q_rep = jnp.broadcast_to(q_row, (dh, dh)).astype(BF16)
        raw.append((_dot_nt(kn_ref[0, :, sl].astype(BF16), q_rep),
                    [_dot_nt(kbuf[slot, h, n].astype(BF16), q_rep) for n in range(N_SEL_PAGES)]))
    for h in range(MOBA_N_HEADS):
        sl = slice(h * dh, (h + 1) * dh)
        s_own = jnp.where(own_ok, raw[h][0] * scale, NEG_BIG)
        s_parts = [sp * scale for sp in raw[h][1]]
        m = jnp.max(s_own, axis=0, keepdims=True)
        for sp in s_parts:
            m = jnp.maximum(m, jnp.max(sp, axis=0, keepdims=True))
        p_own = jnp.exp(s_own - m)
        l = jnp.sum(p_own, axis=0, keepdims=True)
        acc = jnp.sum(p_own * vn_ref[0, :, sl], axis=0, keepdims=True)
        for n, sp in enumerate(s_parts):
            pp = jnp.exp(sp - m)
            l = l + jnp.sum(pp, axis=0, keepdims=True)
            acc = acc + jnp.sum(pp * vbuf[slot, h, n], axis=0, keepdims=True)
        o_ref[0, :, sl] = jnp.where(is_t, acc / l, o_ref[0, :, sl])


def _moba_sample_call(page_table, sel, p3, k3, v3, cache_k, cache_v, layer, *, t_valid):
    bs, rows, _ = p3.shape
    W = MOBA_WIDTH

    def new_spec(col):
        return pl.BlockSpec((1, rows, W), lambda s, pt, sl: (s // t_valid, 0, col // W))

    buf_shape = (2, MOBA_N_HEADS, N_SEL_PAGES, PAGE_SIZE, MOBA_HEAD_DIM)
    grid_spec = pltpu.PrefetchScalarGridSpec(
        num_scalar_prefetch=2,
        grid=(bs * t_valid,),
        in_specs=[new_spec(COL_Q), new_spec(0), new_spec(0),
                  pl.BlockSpec(memory_space=pl.ANY), pl.BlockSpec(memory_space=pl.ANY)],
        out_specs=pl.BlockSpec((1, rows, W), lambda s, pt, sl: (s // t_valid, 0, 0)),
        scratch_shapes=[pltpu.VMEM(buf_shape, F32), pltpu.VMEM(buf_shape, F32),
                        pltpu.SemaphoreType.DMA((2,))],
    )
    return pl.pallas_call(
        functools.partial(_moba_sample_kernel, layer=layer, t_valid=t_valid),
        grid_spec=grid_spec,
        out_shape=_sds((bs, rows, W)),
        compiler_params=_cparams(("arbitrary",)),
        name="moba_sample",
    )(page_table, sel.reshape(-1), p3, k3, v3, cache_k, cache_v)


def _merge_kernel(ys_ref, ym_ref, yl_ref, g0_ref, g1_ref, g2_ref, x_ref, gt_ref, gp_ref,
                  ws_ref, wm_ref, wl_ref, wo_ref, o_ref):
    merged = (jax.nn.sigmoid(g0_ref[...]) * _dot(ys_ref[...].astype(BF16), ws_ref[...])
              + jax.nn.sigmoid(g1_ref[...]) * _dot(ym_ref[...].astype(BF16), wm_ref[...])
              + jax.nn.sigmoid(g2_ref[...]) * _dot(yl_ref[...].astype(BF16), wl_ref[...]))
    mix = _dot(merged.astype(BF16), wo_ref[...])
    o_ref[...] = x_ref[...] + gt_ref[0] * (_rms(mix) * gp_ref[...])


def _merge_call(ys, ym, yl, p2, x2, gt, gp, ws, wm, wl, wo, *, tm, tiles_per_mod):
    rows, d = x2.shape
    mod_rows = gt.shape[1]
    row_spec = pl.BlockSpec((tm, d), lambda i: (i, 0))
    w_spec = pl.BlockSpec((d, d), lambda i: (0, 0), pipeline_mode=pl.Buffered(1))

    def gate_spec(n):
        return pl.BlockSpec((tm, d), lambda i: (i, COL_GATE // d + n))

    return pl.pallas_call(
        _merge_kernel,
        grid=(rows // tm,),
        in_specs=[row_spec, row_spec, row_spec, gate_spec(0), gate_spec(1), gate_spec(2), row_spec,
                  pl.BlockSpec((1, mod_rows, d), lambda i: (i // tiles_per_mod, 0, 0)),
                  pl.BlockSpec((1, d), lambda i: (0, 0)),
                  w_spec, w_spec, w_spec, w_spec],
        out_specs=row_spec,
        out_shape=_sds((rows, d)),
        compiler_params=_cparams(("arbitrary",)),
        name="merge",
    )(ys, ym, yl, p2, p2, p2, x2, gt, gp, ws, wm, wl, wo)


def _mlp_kernel(x_ref, sc_ref, sh_ref, gt_ref, g1_ref, g2_ref, wu_ref, wd_ref, o_ref):
    x = x_ref[...]
    h2 = (_rms(x) * g1_ref[...]) * (1.0 + sc_ref[0]) + sh_ref[0]
    up = jnp.maximum(_dot(h2.astype(BF16), wu_ref[...]), 0.0)
    f = _dot((up * up).astype(BF16), wd_ref[...])
    o_ref[...] = x + gt_ref[0] * (_rms(f) * g2_ref[...])


def _mlp_call(x2, sc, sh, gt, g1, g2, wu, wd, *, tm, tiles_per_mod):
    rows, d = x2.shape
    hid = wu.shape[1]
    mod_rows = sc.shape[1]
    row_spec = pl.BlockSpec((tm, d), lambda i: (i, 0))
    mod_spec = pl.BlockSpec((1, mod_rows, d), lambda i: (i // tiles_per_mod, 0, 0))
    vec_spec = pl.BlockSpec((1, d), lambda i: (0, 0))
    return pl.pallas_call(
        _mlp_kernel,
        grid=(rows // tm,),
        in_specs=[row_spec, mod_spec, mod_spec, mod_spec, vec_spec, vec_spec,
                  pl.BlockSpec((d, hid), lambda i: (0, 0), pipeline_mode=pl.Buffered(1)),
                  pl.BlockSpec((hid, d), lambda i: (0, 0), pipeline_mode=pl.Buffered(1))],
        out_specs=row_spec,
        out_shape=_sds((rows, d)),
        compiler_params=_cparams(("arbitrary",)),
        name="mlp",
    )(x2, sc, sh, gt, g1, g2, wu, wd)


def _block_diag_tiles(w, per_tile):
    nblk, d, _ = w.shape
    wt = w.reshape(nblk // per_tile, per_tile, d, d)
    eye = jnp.eye(per_tile, dtype=w.dtype)
    bd = wt[:, :, :, None, :] * eye[None, :, None, :, None]
    return bd.reshape(nblk // per_tile, per_tile * d, per_tile * d)


def _layer_weights(l, w_in, w_branch_ssd, w_branch_moba, w_branch_lru, w_out, w_up, w_down,
                   lru_wa, lru_wx):
    o = [0]
    for s in (SSD_D_INNER, SSD_CONV_DIM, SSD_N_HEADS, MOBA_WIDTH, MOBA_WIDTH, MOBA_WIDTH,
              LRU_WIDTH, LRU_WIDTH, N_BRANCH * D_MODEL):
        o.append(o[-1] + s)
    w = w_in[l]
    seg = lambda n: w[:, o[n]:o[n + 1]]
    w_all = jnp.concatenate([seg(1), seg(0), seg(3), seg(6), seg(7), seg(8), seg(4), seg(5)],
                            axis=1).astype(BF16)
    wdt = jnp.pad(seg(2), ((0, 0), (0, DT_PAD - SSD_N_HEADS))).astype(BF16)
    per_tile = 256 // LRU_BLOCK
    return dict(
        w_all=w_all, wdt=wdt,
        ws=w_branch_ssd[l].astype(BF16), wm=w_branch_moba[l].astype(BF16),
        wl=w_branch_lru[l].astype(BF16), wo=w_out[l].astype(BF16),
        wu=w_up[l].astype(BF16), wd=w_down[l].astype(BF16),
        wa_bd=_block_diag_tiles(lru_wa[l], per_tile).astype(BF16),
        wx_bd=_block_diag_tiles(lru_wx[l], per_tile).astype(BF16),
    )


def _row(v):
    return v.reshape(1, -1)


def _pad_lanes(v, width):
    return jnp.pad(v, (0, width - v.shape[0])).reshape(1, width)


def _halo(buf):
    return jnp.pad(buf, ((0, 0), (CONV_HALO - buf.shape[1], 0), (0, 0)))


def _group_layer(x3, mods, wts, prm, ssd_buf, ssd_h, lru_buf, lru_h, moba_fn, *, t_valid, tm_proj,
                 tm_dense, ssd_rows, lru_tm, rows_per_mod):
    bsz, t_pad, d = x3.shape
    rows = bsz * t_pad
    x2 = x3.reshape(rows, d)
    tiles_per_mod = rows_per_mod // tm_dense
    sh1, sc1, gt1, sh2, sc2, gt2 = mods

    p2, dt2, k2, v2 = _inproj_call(x2, sc1, sh1, _row(prm['g_pre_mix']), wts['w_all'], wts['wdt'],
                                   tm=tm_proj, tiles_per_mod=rows_per_mod // tm_proj)
    p3 = p2.reshape(bsz, t_pad, P_WIDTH)
    dt3 = dt2.reshape(bsz, t_pad, DT_PAD)
    k3 = k2.reshape(bsz, t_pad, MOBA_WIDTH)
    v3 = v2.reshape(bsz, t_pad, MOBA_WIDTH)

    y_ssd, ssd_h_new = _ssd_call(
        p3, dt3, _halo(ssd_buf), ssd_h.reshape(bsz, SSD_N_HEADS * SSD_HEAD_DIM, SSD_D_STATE),
        prm['ssd_conv_w'], _row(prm['ssd_conv_b']), _pad_lanes(prm['ssd_dt_bias'], DT_PAD),
        _pad_lanes(prm['ssd_a_log'], DT_PAD), _row(jnp.repeat(prm['ssd_d'], SSD_HEAD_DIM)),
        _row(prm['ssd_norm_g']), rows_in=ssd_rows, t_valid=t_valid)

    y_lru, lru_h_new = _lru_call(
        p3, _halo(lru_buf), lru_h.reshape(bsz, 1, LRU_WIDTH), prm['lru_conv_w'],
        _row(prm['lru_conv_b']), wts['wa_bd'], wts['wx_bd'], _row(prm['lru_ba']),
        _row(prm['lru_bx']), _row(prm['lru_lambda']), tm=lru_tm, t_valid=t_valid)

    y_moba = moba_fn(p3, k3, v3)

    x1 = _merge_call(y_ssd.reshape(rows, d), y_moba.reshape(rows, d), y_lru.reshape(rows, d), p2, x2,
                     gt1, _row(prm['g_post_mix']), wts['ws'], wts['wm'], wts['wl'], wts['wo'],
                     tm=tm_dense, tiles_per_mod=tiles_per_mod)
    x_out = _mlp_call(x1, sc2, sh2, gt2, _row(prm['g_pre_mlp']), _row(prm['g_post_mlp']),
                      wts['wu'], wts['wd'], tm=tm_dense, tiles_per_mod=tiles_per_mod)

    lo = t_valid - (SSD_CONV - 1)
    k_new = k3[:, :t_valid].reshape(bsz, t_valid, MOBA_N_HEADS, MOBA_HEAD_DIM)
    v_new = v3[:, :t_valid].reshape(bsz, t_valid, MOBA_N_HEADS, MOBA_HEAD_DIM)
    ssd_buf_new = p3[:, lo:t_valid, COL_XBC:COL_XBC + SSD_CONV_DIM]
    lru_buf_new = p3[:, lo:t_valid, COL_XR:COL_XR + LRU_WIDTH]
    return (x_out.reshape(bsz, t_pad, d), k_new, v_new, ssd_buf_new,
            ssd_h_new.reshape(bsz, SSD_N_HEADS, SSD_HEAD_DIM, SSD_D_STATE), lru_buf_new,
            lru_h_new.reshape(bsz, LRU_WIDTH))


def kernel(x_prompt, x_sample, c_prompt, c_sample, cache_k, cache_v, page_table, state_ssm, state_ssm_conv, state_lru, state_lru_conv, w_mod, b_mod, g_pre_mix, g_post_mix, g_pre_mlp, g_post_mlp, w_in, ssd_conv_w, ssd_conv_b, ssd_dt_bias, ssd_a_log, ssd_d, ssd_norm_g, lru_conv_w, lru_conv_b, lru_wa, lru_ba, lru_wx, lru_bx, lru_lambda, w_branch_ssd, w_branch_moba, w_branch_lru, w_out, w_up, w_down):
    bp, tp, d = x_prompt.shape
    bs, ts, _ = x_sample.shape
    depth = w_in.shape[0]
    n_pool = cache_k.shape[1]
    assert ts <= SAMPLE_T_PAD and ts >= SSD_CONV - 1
    assert (page_table.shape[1] * PAGE_SIZE) % MOBA_BLOCK == 0
    assert tp % SSD_CHUNK == 0 and tp % MOBA_BLOCK == 0

    tm_p = min(1024, tp)
    tm_dense_p = min(512, tp)
    lru_tm_p = min(256, tp)

    n_c = bp + bs
    n_c_pad = -(-n_c // SUBLANES) * SUBLANES
    c_all = jnp.pad(jnp.concatenate([c_prompt, c_sample], axis=0), ((0, n_c_pad - n_c), (0, 0)))
    mod_all = _mod_call(c_all, w_mod, b_mod)

    xs_pad = jnp.pad(x_sample, ((0, 0), (0, SAMPLE_T_PAD - ts), (0, 0)))

    zero_ssd_buf = jnp.zeros((bp, SSD_CONV - 1, SSD_CONV_DIM), F32)
    zero_ssd_h = jnp.zeros((bp, SSD_N_HEADS, SSD_HEAD_DIM, SSD_D_STATE), F32)
    zero_lru_buf = jnp.zeros((bp, LRU_CONV - 1, LRU_WIDTH), F32)
    zero_lru_h = jnp.zeros((bp, LRU_WIDTH), F32)

    outs_p = [[] for _ in range(6)]
    outs_s = [[] for _ in range(6)]
    xp, xs = x_prompt, xs_pad
    for l in range(depth):
        prm = {
            'g_pre_mix': g_pre_mix[l], 'g_post_mix': g_post_mix[l], 'g_pre_mlp': g_pre_mlp[l],
            'g_post_mlp': g_post_mlp[l], 'ssd_conv_w': ssd_conv_w[l], 'ssd_conv_b': ssd_conv_b[l],
            'ssd_dt_bias': ssd_dt_bias[l], 'ssd_a_log': ssd_a_log[l], 'ssd_d': ssd_d[l],
            'ssd_norm_g': ssd_norm_g[l], 'lru_conv_w': lru_conv_w[l], 'lru_conv_b': lru_conv_b[l],
            'lru_ba': lru_ba[l].reshape(-1), 'lru_bx': lru_bx[l].reshape(-1),
            'lru_lambda': lru_lambda[l],
        }
        wts = _layer_weights(l, w_in, w_branch_ssd, w_branch_moba, w_branch_lru, w_out, w_up,
                             w_down, lru_wa, lru_wx)
        mod_l = mod_all[l]
        mods_p = [mod_l[:bp, n * d:(n + 1) * d].reshape(bp, 1, d) for n in range(6)]
        mods_s = [jnp.repeat(mod_l[bp:bp + bs, n * d:(n + 1) * d], SAMPLE_T_PAD, axis=0)
                  .reshape(1, bs * SAMPLE_T_PAD, d) for n in range(6)]

        ksum_box = []

        def moba_prompt(p3, k3, v3, l=l, ksum_box=ksum_box):
            y, ksum = _moba_prompt_call(p3, k3, v3, page_table, cache_k, l)
            ksum_box.append(ksum)
            return y

        res_p = _group_layer(xp, mods_p, wts, prm, zero_ssd_buf, zero_ssd_h, zero_lru_buf, zero_lru_h,
                             moba_prompt, t_valid=tp, tm_proj=tm_p, tm_dense=tm_dense_p,
                             ssd_rows=SSD_CHUNK, lru_tm=lru_tm_p, rows_per_mod=tp)
        xp = res_p[0]
        for n in range(6):
            outs_p[n].append(res_p[n + 1])

        def moba_sample(p3, k3, v3, l=l, ksum_box=ksum_box):
            sel = _select_call(p3, ksum_box[0])
            sel = sel[:, :ts, :MOBA_N_HEADS * 4].reshape(bs, ts, MOBA_N_HEADS, 4)[..., :MOBA_TOPK]
            return _moba_sample_call(page_table, sel, p3, k3, v3, cache_k, cache_v, l, t_valid=ts)

        res_s = _group_layer(xs, mods_s, wts, prm, state_ssm_conv[l], state_ssm[l], state_lru_conv[l],
                             state_lru[l], moba_sample, t_valid=ts, tm_proj=bs * SAMPLE_T_PAD,
                             tm_dense=bs * SAMPLE_T_PAD, ssd_rows=SAMPLE_T_PAD, lru_tm=SAMPLE_T_PAD,
                             rows_per_mod=bs * SAMPLE_T_PAD)
        xs = res_s[0]
        for n in range(6):
            outs_s[n].append(res_s[n + 1])

    st = lambda lst: jnp.stack(lst)
    kp, vp, ssmc_p, ssm_p, lruc_p, lru_p = outs_p
    ks, vs, ssmc_s, ssm_s, lruc_s, lru_s = outs_s
    return (xp, xs[:, :ts],
            st(kp), st(vp), st(ks), st(vs),
            st(ssm_p), st(ssmc_p), st(lru_p), st(lruc_p),
            st(ssm_s), st(ssmc_s), st(lru_s), st(lruc_s))
```

```python
import functools
import math

import jax
import jax.numpy as jnp
from jax import lax
from jax.experimental import pallas as pl
from jax.experimental.pallas import tpu as pltpu

F32 = jnp.float32
BF16 = jnp.bfloat16
HIGHEST = lax.Precision.HIGHEST

D_MODEL = 1024
PAGE_SIZE = 128
SSD_D_INNER = D_MODEL
SSD_HEAD_DIM = 64
SSD_N_HEADS = SSD_D_INNER // SSD_HEAD_DIM
SSD_N_GROUPS = 4
SSD_D_STATE = 128
SSD_CONV = 4
SSD_CHUNK = 128
SSD_CONV_DIM = SSD_D_INNER + 2 * SSD_N_GROUPS * SSD_D_STATE
MOBA_N_HEADS = 8
MOBA_HEAD_DIM = 128
MOBA_WIDTH = MOBA_N_HEADS * MOBA_HEAD_DIM
MOBA_BLOCK = 256
MOBA_TOPK = 3
LRU_WIDTH = D_MODEL
LRU_N_BLOCKS = 16
LRU_BLOCK = LRU_WIDTH // LRU_N_BLOCKS
LRU_CONV = 4
LRU_C = 8.0
N_BRANCH = 3
MLP_HIDDEN = 4 * D_MODEL
RMS_EPS = 1e-6

SUBLANES = 8
LANES = 128
VMEM_LIMIT_BYTES = 56 * 1024 * 1024

SAMPLE_T_PAD = SUBLANES
CONV_HALO = SUBLANES
NEG_BIG = -1e30
LOG2E = math.log2(math.e)

COL_XBC = 0
COL_Z = COL_XBC + SSD_CONV_DIM
COL_Q = COL_Z + SSD_D_INNER
COL_XR = COL_Q + MOBA_WIDTH
COL_GR = COL_XR + LRU_WIDTH
COL_GATE = COL_GR + LRU_WIDTH
P_WIDTH = COL_GATE + N_BRANCH * D_MODEL
PROJ_TN = 1024
P_TILES = P_WIDTH // PROJ_TN
DT_PAD = LANES


def _cparams(sem):
    return pltpu.CompilerParams(dimension_semantics=sem, vmem_limit_bytes=VMEM_LIMIT_BYTES)


def _sds(shape, dtype=F32):
    return jax.ShapeDtypeStruct(shape, dtype)


def _silu(x):
    return x * jax.nn.sigmoid(x)


def _softplus(x):
    return jnp.maximum(x, 0.0) + jnp.log1p(jnp.exp(-jnp.abs(x)))


def _rms(x):
    return x * lax.rsqrt(jnp.mean(x * x, axis=-1, keepdims=True) + RMS_EPS)


def _dot(a, b):
    return jnp.dot(a, b, preferred_element_type=F32)


def _dot_nt(a, b, precision=None):
    return lax.dot_general(a, b, (((1,), (1,)), ((), ())), precision=precision,
                           preferred_element_type=F32)


def _mod_kernel(c_ref, w_ref, b_ref, o_ref):
    s = _silu(c_ref[...]).astype(BF16)
    o_ref[0] = _dot(s, w_ref[0].astype(BF16)) + b_ref[0]


def _mod_call(c_all, w_mod, b_mod):
    depth, d, n = w_mod.shape
    rows = c_all.shape[0]
    tn = 1536
    return pl.pallas_call(
        _mod_kernel,
        grid=(depth, n // tn),
        in_specs=[pl.BlockSpec((rows, d), lambda l, j: (0, 0)),
                  pl.BlockSpec((1, d, tn), lambda l, j: (l, 0, j)),
                  pl.BlockSpec((1, 1, tn), lambda l, j: (l, 0, j))],
        out_specs=pl.BlockSpec((1, rows, tn), lambda l, j: (l, 0, j)),
        out_shape=_sds((depth, rows, n)),
        compiler_params=_cparams(("arbitrary", "arbitrary")),
        name="mod",
    )(c_all, w_mod, b_mod.reshape(depth, 1, n))


def _inproj_kernel(x_ref, sc_ref, sh_ref, g_ref, w_ref, wdt_ref, p_ref, dt_ref, k_ref, v_ref, h_scr):
    j = pl.program_id(1)

    @pl.when(j == 0)
    def _():
        y = _rms(x_ref[...]) * g_ref[...]
        hb = (y * (1.0 + sc_ref[0]) + sh_ref[0]).astype(BF16)
        h_scr[...] = hb
        dt_ref[...] = _dot(hb, wdt_ref[...])

    @pl.when(j < P_TILES)
    def _():
        p_ref[...] = _dot(h_scr[...], w_ref[...])

    @pl.when(j == P_TILES)
    def _():
        k_ref[...] = _dot(h_scr[...], w_ref[...])

    @pl.when(j == P_TILES + 1)
    def _():
        v_ref[...] = _dot(h_scr[...], w_ref[...])


def _inproj_call(x2, sc, sh, g, w_all, wdt, *, tm, tiles_per_mod):
    rows, d = x2.shape
    assert PROJ_TN == MOBA_WIDTH and w_all.shape[1] == P_WIDTH + 2 * MOBA_WIDTH
    mod_rows = sc.shape[1]
    mod_spec = pl.BlockSpec((1, mod_rows, d), lambda i, j: (i // tiles_per_mod, 0, 0))
    kv_spec = pl.BlockSpec((tm, MOBA_WIDTH), lambda i, j: (i, 0))
    return pl.pallas_call(
        _inproj_kernel,
        grid=(rows // tm, P_TILES + 2),
        in_specs=[pl.BlockSpec((tm, d), lambda i, j: (i, 0)),
                  mod_spec, mod_spec,
                  pl.BlockSpec((1, d), lambda i, j: (0, 0)),
                  pl.BlockSpec((d, PROJ_TN), lambda i, j: (0, j)),
                  pl.BlockSpec((d, DT_PAD), lambda i, j: (0, 0))],
        out_specs=[pl.BlockSpec((tm, PROJ_TN), lambda i, j: (i, jnp.minimum(j, P_TILES - 1))),
                   pl.BlockSpec((tm, DT_PAD), lambda i, j: (i, 0)),
                   kv_spec, kv_spec],
        out_shape=[_sds((rows, P_WIDTH)), _sds((rows, DT_PAD)),
                   _sds((rows, MOBA_WIDTH)), _sds((rows, MOBA_WIDTH))],
        scratch_shapes=[pltpu.VMEM((tm, d), BF16)],
        compiler_params=_cparams(("arbitrary", "arbitrary")),
        name="inproj",
    )(x2, sc, sh, g, w_all, wdt)


def _conv_tile(prev_scr, x, buf_ref, w_ref, b_ref, tile_idx):
    rows, width = x.shape

    @pl.when(tile_idx == 0)
    def _():
        prev_scr[...] = buf_ref[0]

    prev = prev_scr[...]
    prev_scr[...] = x[rows - CONV_HALO:, :]
    n_groups = rows // CONV_HALO
    x3 = jnp.concatenate([prev, x], axis=0).reshape(n_groups + 1, CONV_HALO, width)
    row = lax.broadcasted_iota(jnp.int32, (1, CONV_HALO, width), 1)
    acc = b_ref[...] + w_ref[SSD_CONV - 1:SSD_CONV, :] * x
    for k in range(1, SSD_CONV):
        r3 = pltpu.roll(x3, k, 1)
        shifted = jnp.where(row >= k, r3[1:], r3[:-1]).reshape(rows, width)
        acc = acc + w_ref[SSD_CONV - 1 - k:SSD_CONV - k, :] * shifted
    return acc


def _ssd_kernel(xbc_ref, z_ref, dt_ref, buf_ref, h0_ref, cw_ref, cb_ref, dtb_ref, alog_ref,
                dsk_ref, ng_ref, y_ref, h_ref, xp_scr, dt_scr, *, rows_in, t_valid):
    L = SSD_CHUNK
    N = SSD_D_STATE
    P2 = 2 * SSD_HEAD_DIM
    c = pl.program_id(1)

    @pl.when(c == 0)
    def _():
        h_ref[0] = h0_ref[0]

    xc = _silu(_conv_tile(xp_scr, xbc_ref[0], buf_ref, cw_ref, cb_ref, c))
    if rows_in < L:
        xc = jnp.concatenate([xc, jnp.zeros((L - rows_in, SSD_CONV_DIM), F32)], axis=0)
    xs = xc[:, :SSD_D_INNER]
    bm = xc[:, SSD_D_INNER:SSD_D_INNER + SSD_N_GROUPS * N]
    cm = xc[:, SSD_D_INNER + SSD_N_GROUPS * N:]

    if rows_in < L:
        dt_scr[...] = jnp.zeros((L, LANES), F32)
    dt_scr[0:rows_in, :] = dt_ref[0]
    lane = lax.broadcasted_iota(jnp.int32, (L, LANES), 1)
    row = lax.broadcasted_iota(jnp.int32, (L, LANES), 0)
    dtv = _softplus(dt_scr[...] + dtb_ref[...])
    a = jnp.where(lane < SSD_N_HEADS, -jnp.exp(alog_ref[...]), 0.0)
    dtv = jnp.where((row + c * L < t_valid) & (lane < SSD_N_HEADS), dtv, 0.0)
    dta = dtv * a
    ri = lax.broadcasted_iota(jnp.int32, (L, L), 0)
    ci = lax.broadcasted_iota(jnp.int32, (L, L), 1)
    causal = ri >= ci
    cum = jnp.dot(causal.astype(F32), dta, precision=HIGHEST, preferred_element_type=F32)
    cum_t = cum.T
    dtv_t = dtv.T
    ecum = jnp.exp(cum)
    cum_last = cum[L - 1:L, :]
    wgt = jnp.exp(cum_last - cum) * dtv
    cdec_t = jnp.exp(cum_t[:, L - 1:L])
    left = lax.broadcasted_iota(jnp.int32, (L, P2), 1) < SSD_HEAD_DIM
    left_rows = lax.broadcasted_iota(jnp.int32, (P2, N), 0) < SSD_HEAD_DIM

    y_tiles = []
    for j in range(SSD_N_HEADS // 2):
        g = j // (SSD_N_HEADS // SSD_N_GROUPS // 2)
        h_a, h_b = 2 * j, 2 * j + 1
        bm_g = bm[:, g * N:(g + 1) * N]
        cm_g = cm[:, g * N:(g + 1) * N].astype(BF16)
        cb = _dot_nt(cm_g, bm_g.astype(BF16))
        xs_p = xs[:, j * P2:(j + 1) * P2]
        m_parts = []
        for hh in (h_a, h_b):
            seg = cum[:, hh:hh + 1] - cum_t[hh:hh + 1, :]
            dec = jnp.exp(jnp.where(causal, seg, -jnp.inf))
            m_parts.append((cb * dec * dtv_t[hh:hh + 1, :]).astype(BF16))
        m_pair = jnp.concatenate(m_parts, axis=1)
        x_bd = jnp.concatenate([jnp.where(left, xs_p, 0.0), jnp.where(left, 0.0, xs_p)],
                               axis=0).astype(BF16)
        y_diag = _dot(m_pair, x_bd)
        h_prev = h_ref[0, j * P2:(j + 1) * P2, :]
        y_off = _dot_nt(cm_g, h_prev.astype(BF16))
        e_sel = jnp.where(left, ecum[:, h_a:h_a + 1], ecum[:, h_b:h_b + 1])
        y_tiles.append(y_diag + y_off * e_sel + dsk_ref[:, j * P2:(j + 1) * P2] * xs_p)
        w_sel = jnp.where(left, wgt[:, h_a:h_a + 1], wgt[:, h_b:h_b + 1])
        xw_t = (xs_p * w_sel).T.astype(BF16)
        st = _dot(xw_t, bm_g.astype(BF16))
        d_sel = jnp.where(left_rows, cdec_t[h_a:h_a + 1, :], cdec_t[h_b:h_b + 1, :])
        h_ref[0, j * P2:(j + 1) * P2, :] = h_prev * d_sel + st

    gw = SSD_D_INNER // SSD_N_GROUPS
    tiles_per_group = gw // P2
    zs = _silu(z_ref[0])
    for g in range(SSD_N_GROUPS):
        yg = jnp.concatenate(y_tiles[g * tiles_per_group:(g + 1) * tiles_per_group], axis=1)
        yg = yg[0:rows_in, :] * zs[:, g * gw:(g + 1) * gw]
        y_ref[0, :, g * gw:(g + 1) * gw] = (_rms(yg) * ng_ref[:, g * gw:(g + 1) * gw]).astype(BF16)


def _ssd_call(p3, dt3, buf8, h0, cw, cb, dtb, alog, dsk, ng, *, rows_in, t_valid):
    bsz, t_pad, _ = p3.shape
    nc = t_pad // rows_in
    L = SSD_CHUNK
    hp = SSD_N_HEADS * SSD_HEAD_DIM
    const2 = lambda b, c: (0, 0)
    kern = functools.partial(_ssd_kernel, rows_in=rows_in, t_valid=t_valid)
    return pl.pallas_call(
        kern,
        grid=(bsz, nc),
        in_specs=[pl.BlockSpec((1, rows_in, SSD_CONV_DIM), lambda b, c: (b, c, COL_XBC // SSD_CONV_DIM)),
                  pl.BlockSpec((1, rows_in, SSD_D_INNER), lambda b, c: (b, c, COL_Z // SSD_D_INNER)),
                  pl.BlockSpec((1, rows_in, DT_PAD), lambda b, c: (b, c, 0)),
                  pl.BlockSpec((1, CONV_HALO, SSD_CONV_DIM), lambda b, c: (b, 0, 0)),
                  pl.BlockSpec((1, hp, SSD_D_STATE), lambda b, c: (b, 0, 0)),
                  pl.BlockSpec((SSD_CONV, SSD_CONV_DIM), const2),
                  pl.BlockSpec((1, SSD_CONV_DIM), const2),
                  pl.BlockSpec((1, DT_PAD), const2),
                  pl.BlockSpec((1, DT_PAD), const2),
                  pl.BlockSpec((1, SSD_D_INNER), const2),
                  pl.BlockSpec((1, SSD_D_INNER), const2)],
        out_specs=[pl.BlockSpec((1, rows_in, SSD_D_INNER), lambda b, c: (b, c, 0)),
                   pl.BlockSpec((1, hp, SSD_D_STATE), lambda b, c: (b, 0, 0))],
        out_shape=[_sds((bsz, t_pad, SSD_D_INNER), BF16), _sds((bsz, hp, SSD_D_STATE))],
        scratch_shapes=[pltpu.VMEM((CONV_HALO, SSD_CONV_DIM), F32),
                        pltpu.VMEM((L, DT_PAD), F32)],
        compiler_params=_cparams(("arbitrary", "arbitrary")),
        name="ssd",
    )(p3, p3, dt3, buf8, h0, cw, cb, dtb, alog, dsk, ng)


def _lru_kernel(xr_ref, gr_ref, buf_ref, h0_ref, cw_ref, cb_ref, wa_ref, wx_ref, ba_ref, bx_ref,
                lam_ref, y_ref, h_ref, xp_scr, *, tm, t_valid, t_pad):
    i = pl.program_id(1)

    @pl.when(i == 0)
    def _():
        h_ref[0] = h0_ref[0]

    xc = _conv_tile(xp_scr, xr_ref[0], buf_ref, cw_ref, cb_ref, i)
    bw = wa_ref.shape[1]
    r_parts, i_parts = [], []
    for j in range(LRU_WIDTH // bw):
        xj = xc[:, j * bw:(j + 1) * bw].astype(BF16)
        r_parts.append(_dot(xj, wa_ref[j]))
        i_parts.append(_dot(xj, wx_ref[j]))
    r = jax.nn.sigmoid(jnp.concatenate(r_parts, axis=1) + ba_ref[...])
    ig = jax.nn.sigmoid(jnp.concatenate(i_parts, axis=1) + bx_ref[...])
    log_a = (-LRU_C) * r * _softplus(-lam_ref[...])
    a_full = jnp.exp(log_a)
    b_full = jnp.sqrt(-jnp.tanh(log_a) * (a_full * a_full + 1.0)) * (ig * xc)
    gate = jax.nn.gelu(gr_ref[0], approximate=True)

    n_groups = tm // SUBLANES
    row = lax.broadcasted_iota(jnp.int32, (tm, LANES), 0)
    row_in_group = lax.broadcasted_iota(jnp.int32, (1, SUBLANES, LANES), 1)
    for s in range(LRU_WIDTH // LANES):
        sl = slice(s * LANES, (s + 1) * LANES)
        a, b = a_full[:, sl], b_full[:, sl]
        if t_valid < t_pad:
            valid = row + i * tm < t_valid
            a = jnp.where(valid, a, 1.0)
            b = jnp.where(valid, b, 0.0)
        a = a.reshape(n_groups, SUBLANES, LANES)
        b = b.reshape(n_groups, SUBLANES, LANES)
        k = 1
        while k < SUBLANES:
            keep = row_in_group >= k
            b = b + a * jnp.where(keep, pltpu.roll(b, k, 1), 0.0)
            a = a * jnp.where(keep, pltpu.roll(a, k, 1), 1.0)
            k *= 2
        carry = h_ref[0, :, sl]
        groups = []
        for g in range(n_groups):
            hg = b[g] + a[g] * carry
            groups.append(hg)
            carry = hg[SUBLANES - 1:SUBLANES, :]
        h = groups[0] if n_groups == 1 else jnp.concatenate(groups, axis=0)
        y_ref[0, :, sl] = (h * gate[:, sl]).astype(BF16)
        h_ref[0, :, sl] = carry


def _lru_call(p3, buf8, h0, cw, cb, wa_bd, wx_bd, ba, bx, lam, *, tm, t_valid):
    bsz, t_pad, _ = p3.shape
    W = LRU_WIDTH
    nb, bw, _ = wa_bd.shape
    const2 = lambda b, i: (0, 0)
    const3 = lambda b, i: (0, 0, 0)
    kern = functools.partial(_lru_kernel, tm=tm, t_valid=t_valid, t_pad=t_pad)
    return pl.pallas_call(
        kern,
        grid=(bsz, t_pad // tm),
        in_specs=[pl.BlockSpec((1, tm, W), lambda b, i: (b, i, COL_XR // W)),
                  pl.BlockSpec((1, tm, W), lambda b, i: (b, i, COL_GR // W)),
                  pl.BlockSpec((1, CONV_HALO, W), lambda b, i: (b, 0, 0)),
                  pl.BlockSpec((1, 1, W), lambda b, i: (b, 0, 0)),
                  pl.BlockSpec((LRU_CONV, W), const2),
                  pl.BlockSpec((1, W), const2),
                  pl.BlockSpec((nb, bw, bw), const3),
                  pl.BlockSpec((nb, bw, bw), const3),
                  pl.BlockSpec((1, W), const2),
                  pl.BlockSpec((1, W), const2),
                  pl.BlockSpec((1, W), const2)],
        out_specs=[pl.BlockSpec((1, tm, W), lambda b, i: (b, i, 0)),
                   pl.BlockSpec((1, 1, W), lambda b, i: (b, 0, 0))],
        out_shape=[_sds((bsz, t_pad, W), BF16), _sds((bsz, 1, W))],
        scratch_shapes=[pltpu.VMEM((CONV_HALO, W), F32)],
        compiler_params=_cparams(("arbitrary", "arbitrary")),
        name="lru",
    )(p3, p3, buf8, h0, cw, cb, wa_bd, wx_bd, ba, bx, lam)


def _topk_mask(gate, n_valid, col):
    nb = gate.shape[1]
    past = col < n_valid
    gm = jnp.where(past, gate, -jnp.inf)
    rank = jnp.zeros(gate.shape, F32)
    for j in range(nb):
        gj = gm[:, j:j + 1]
        ahead = (gj > gm) | ((gj == gm) & (col > j))
        rank = rank + jnp.where(ahead, 1.0, 0.0)
    return (rank < MOBA_TOPK) & past, rank


PAGES_PER_BLOCK = MOBA_BLOCK // PAGE_SIZE


def _ksum_stream(pt_ref, ck_hbm, ks_ref, kbuf, sem, step, n_steps, *, layer, pps, n_pages):
    def copies(st, slot, lookup):
        out = []
        for u in range(pps):
            if lookup:
                p = st * pps + u
                page = pt_ref[lax.div(p, n_pages), lax.rem(p, n_pages)]
            else:
                page = 0
            out.append(pltpu.make_async_copy(ck_hbm.at[layer, page], kbuf.at[slot, u], sem.at[slot]))
        return out

    slot = lax.rem(step, 2)

    @pl.when(step == 0)
    def _():
        for c in copies(step, slot, True):
            c.start()

    @pl.when(step + 1 < n_steps)
    def _():
        for c in copies(step + 1, 1 - slot, True):
            c.start()

    for c in copies(step, slot, False):
        c.wait()

    blk0 = lax.div(lax.rem(step * pps, n_pages), PAGES_PER_BLOCK)

    @pl.when(blk0 == 0)
    def _():
        ks_ref[...] = jnp.zeros(ks_ref.shape, F32)

    sums = []
    for r in range(pps // PAGES_PER_BLOCK):
        acc = jnp.sum(kbuf[slot, r * PAGES_PER_BLOCK], axis=0)
        for u in range(1, PAGES_PER_BLOCK):
            acc = acc + jnp.sum(kbuf[slot, r * PAGES_PER_BLOCK + u], axis=0)
        sums.append(acc)
    row = lax.broadcasted_iota(jnp.int32, ks_ref.shape[2:], 0)
    for h in range(MOBA_N_HEADS):
        cur = ks_ref[0, h]
        for r, acc in enumerate(sums):
            cur = jnp.where(row == blk0 + r, acc[h:h + 1, :], cur)
        ks_ref[0, h] = cur


def _moba_prompt_kernel(pt_ref, q_ref, k_ref, v_ref, ck_hbm, o_ref, ks_ref, kmean_scr, kg_scr, vtg_scr,
                        sel_scr, kbuf, sem, *, nb, grp, nh, layer, pps, n_pages):
    i = pl.program_id(2)
    BS = MOBA_BLOCK
    GB = grp * BS
    dh = MOBA_HEAD_DIM
    scale = MOBA_HEAD_DIM ** -0.5

    step = (pl.program_id(0) * pl.num_programs(1) + pl.program_id(1)) * pl.num_programs(2) + i
    n_steps = pl.num_programs(0) * pl.num_programs(1) * pl.num_programs(2)
    _ksum_stream(pt_ref, ck_hbm, ks_ref, kbuf, sem, step, n_steps, layer=layer, pps=pps,
                 n_pages=n_pages)

    @pl.when(i == 0)
    def _():
        for hh in range(nh):
            hs = slice(hh * dh, (hh + 1) * dh)
            for j in range(nb):
                kmean_scr[hh, j:j + 1, :] = jnp.mean(k_ref[0, j * BS:(j + 1) * BS, hs], axis=0,
                                                     keepdims=True)
            for jj in range(nb // grp):
                kg_scr[hh, jj] = k_ref[0, jj * GB:(jj + 1) * GB, hs].astype(BF16)
                for u in range(grp):
                    r0 = jj * GB + u * BS
                    vtg_scr[hh, jj, :, u * BS:(u + 1) * BS] = v_ref[0, r0:r0 + BS, hs].T.astype(BF16)

    qb_t = []
    for hh in range(nh):
        q_t = q_ref[0, :, hh * dh:(hh + 1) * dh].T
        qb_t.append(q_t.astype(BF16))
        gate = jnp.dot(kmean_scr[hh], q_t, precision=HIGHEST, preferred_element_type=F32)
        blk = lax.broadcasted_iota(jnp.int32, gate.shape, 0)
        past = blk < i
        gm = jnp.where(past, gate, -jnp.inf)
        rank = jnp.zeros(gate.shape, F32)
        for j in range(nb):
            gj = gm[j:j + 1, :]
            ahead = (gj > gm) | ((gj == gm) & (blk > j))
            rank = rank + jnp.where(ahead, 1.0, 0.0)
        sel = jnp.where((rank < MOBA_TOPK) & past, 0.0, -jnp.inf)
        for j in range(nb):
            sel_scr[hh, j] = jnp.broadcast_to(sel[j:j + 1, :], (SUBLANES, BS))

    def block_bias(hh, j):
        rows = sel_scr[hh, j]
        return jnp.broadcast_to(rows[None], (BS // SUBLANES, SUBLANES, BS)).reshape(BS, BS)

    sub = min(MOBA_SUB, grp)
    n_parts = grp // sub
    SB = sub * BS

    def scores(hh, jj, part):
        return _dot(kg_scr[hh, jj, part * SB:(part + 1) * SB, :], qb_t[hh])

    def update(carry, hh, jj, part, s_raw, bias):
        m, l, acc = carry
        s = s_raw * (scale * LOG2E) + bias
        m_new = jnp.maximum(m, jnp.max(s, axis=0, keepdims=True))
        alpha = jnp.exp2(m - m_new)
        p = jnp.exp2(s - m_new)
        l_new = alpha * l + jnp.sum(p, axis=0, keepdims=True)
        v_t = vtg_scr[hh, jj, :, part * SB:(part + 1) * SB]
        acc_new = alpha * acc + _dot(v_t, p.astype(BF16))
        return m_new, l_new, acc_new

    def run_parts(jj, carries, bias_fn, parts):
        s_raw = {part: [scores(hh, jj, part) for hh in range(nh)] for part in parts}
        carries = list(carries)
        for part in parts:
            for hh in range(nh):
                bias = jnp.concatenate(
                    [bias_fn(hh, jj * grp + part * sub + u) for u in range(sub)], axis=0)
                carries[hh] = update(carries[hh], hh, jj, part, s_raw[part][hh], bias)
        return tuple(carries)

    def body(jj, carries):
        return run_parts(jj, carries, block_bias, range(n_parts))

    init = tuple((jnp.full((1, BS), NEG_BIG, F32), jnp.zeros((1, BS), F32), jnp.zeros((dh, BS), F32))
                 for _ in range(nh))
    last = i // grp
    carries = lax.fori_loop(0, last, body, init)

    key = lax.broadcasted_iota(jnp.int32, (BS, BS), 0)
    qry = lax.broadcasted_iota(jnp.int32, (BS, BS), 1)
    causal_bias = jnp.where(key <= qry, 0.0, -jnp.inf)

    def last_bias(hh, j):
        return jnp.where(j == i, causal_bias, block_bias(hh, j))

    carries = run_parts(last, carries, last_bias, [0])
    for part in range(1, n_parts):
        carries = lax.cond(last * grp + part * sub <= i,
                           lambda c, part=part: run_parts(last, c, last_bias, [part]),
                           lambda c: c, carries)
    for hh in range(nh):
        m, l, acc = carries[hh]
        o_ref[0, :, hh * dh:(hh + 1) * dh] = (acc / l).T.astype(BF16)


MOBA_GROUP = 4
MOBA_SUB = 2
MOBA_HEADS_PER_STEP = 4


def _moba_prompt_call(p3, k3, v3, page_table, cache_k, layer):
    bsz, t, _ = p3.shape
    assert t % MOBA_BLOCK == 0 and MOBA_N_HEADS % MOBA_HEADS_PER_STEP == 0
    nb = t // MOBA_BLOCK
    grp = math.gcd(MOBA_GROUP, nb)
    nh = MOBA_HEADS_PER_STEP
    dh = MOBA_HEAD_DIM
    hw = nh * dh
    n_hg = MOBA_N_HEADS // nh
    bs, n_pages = page_table.shape
    n_steps = bsz * n_hg * nb
    pps = (bs * n_pages) // n_steps
    assert pps * n_steps == bs * n_pages and pps % PAGES_PER_BLOCK == 0 and n_pages % pps == 0
    nbs = n_pages // PAGES_PER_BLOCK

    def seq_of(b, h, i):
        return (((b * n_hg + h) * nb + i) * pps) // n_pages

    kern = functools.partial(_moba_prompt_kernel, nb=nb, grp=grp, nh=nh, layer=layer, pps=pps,
                             n_pages=n_pages)
    grid_spec = pltpu.PrefetchScalarGridSpec(
        num_scalar_prefetch=1,
        grid=(bsz, n_hg, nb),
        in_specs=[pl.BlockSpec((1, MOBA_BLOCK, hw), lambda b, h, i, pt: (b, i, COL_Q // hw + h)),
                  pl.BlockSpec((1, t, hw), lambda b, h, i, pt: (b, 0, h), pipeline_mode=pl.Buffered(1)),
                  pl.BlockSpec((1, t, hw), lambda b, h, i, pt: (b, 0, h), pipeline_mode=pl.Buffered(1)),
                  pl.BlockSpec(memory_space=pl.ANY)],
        out_specs=[pl.BlockSpec((1, MOBA_BLOCK, hw), lambda b, h, i, pt: (b, i, h)),
                   pl.BlockSpec((1, MOBA_N_HEADS, nbs, dh), lambda b, h, i, pt: (seq_of(b, h, i), 0, 0, 0))],
        scratch_shapes=[pltpu.VMEM((nh, nb, dh), F32),
                        pltpu.VMEM((nh, nb // grp, grp * MOBA_BLOCK, dh), BF16),
                        pltpu.VMEM((nh, nb // grp, dh, grp * MOBA_BLOCK), BF16),
                        pltpu.VMEM((nh, nb, SUBLANES, MOBA_BLOCK), F32),
                        pltpu.VMEM((2, pps, PAGE_SIZE, MOBA_N_HEADS, dh), F32),
                        pltpu.SemaphoreType.DMA((2,))],
    )
    return pl.pallas_call(
        kern,
        grid_spec=grid_spec,
        out_shape=[_sds((bsz, t, MOBA_WIDTH), BF16), _sds((bs, MOBA_N_HEADS, nbs, dh))],
        compiler_params=_cparams(("arbitrary", "arbitrary", "arbitrary")),
        name="moba_prompt",
    )(page_table, p3, k3, v3, cache_k)


def _select_kernel(q_ref, ks_ref, o_ref, *, nb):
    rows = q_ref.shape[1]
    col = lax.broadcasted_iota(jnp.int32, (rows, nb), 1)
    colf = col.astype(F32)
    lane = lax.broadcasted_iota(jnp.int32, (rows, LANES), 1)
    out = jnp.zeros((rows, LANES), F32)
    for h in range(MOBA_N_HEADS):
        sl = slice(h * MOBA_HEAD_DIM, (h + 1) * MOBA_HEAD_DIM)
        kmean = ks_ref[0, h] * (1.0 / MOBA_BLOCK)
        gate = _dot_nt(q_ref[0, :, sl], kmean, precision=HIGHEST)
        sel, rank = _topk_mask(gate, nb, col)
        for r in range(MOBA_TOPK):
            idx = jnp.sum(jnp.where(sel & (rank == r), colf, 0.0), axis=1, keepdims=True)
            out = jnp.where(lane == h * 4 + r, idx, out)
    o_ref[0] = out.astype(jnp.int32)


def _select_call(p3, ksum):
    bs, rows, _ = p3.shape
    nb = ksum.shape[2]
    assert nb >= MOBA_TOPK
    return pl.pallas_call(
        functools.partial(_select_kernel, nb=nb),
        grid=(bs,),
        in_specs=[pl.BlockSpec((1, rows, MOBA_WIDTH), lambda b: (b, 0, COL_Q // MOBA_WIDTH)),
                  pl.BlockSpec((1, MOBA_N_HEADS, nb, MOBA_HEAD_DIM), lambda b: (b, 0, 0, 0))],
        out_specs=pl.BlockSpec((1, rows, LANES), lambda b: (b, 0, 0)),
        out_shape=_sds((bs, rows, LANES), jnp.int32),
        compiler_params=_cparams(("arbitrary",)),
        name="moba_select",
    )(p3, ksum)


N_SEL_PAGES = MOBA_TOPK * PAGES_PER_BLOCK


def _moba_sample_kernel(pt_ref, sel_ref, q_ref, kn_ref, vn_ref, ck_hbm, cv_hbm, o_ref,
                        kbuf, vbuf, sem, *, layer, t_valid):
    step = pl.program_id(0)
    n_steps = pl.num_programs(0)
    t = step % t_valid
    slot = step % 2
    rows = q_ref.shape[1]
    dh = MOBA_HEAD_DIM
    scale = MOBA_HEAD_DIM ** -0.5

    def gather(st, sl, start):
        b = st // t_valid
        for h in range(MOBA_N_HEADS):
            for r in range(MOBA_TOPK):
                blk = sel_ref[(st * MOBA_N_HEADS + h) * MOBA_TOPK + r] if start else 0
                for half in range(PAGES_PER_BLOCK):
                    page = pt_ref[b, blk * PAGES_PER_BLOCK + half] if start else 0
                    n = r * PAGES_PER_BLOCK + half
                    for src, dst in ((ck_hbm, kbuf), (cv_hbm, vbuf)):
                        cp = pltpu.make_async_copy(src.at[layer, page, :, h, :], dst.at[sl, h, n],
                                                   sem.at[sl])
                        if start:
                            cp.start()
                        else:
                            cp.wait()

    @pl.when(step == 0)
    def _():
        gather(step, slot, True)

    @pl.when(step + 1 < n_steps)
    def _():
        gather(step + 1, 1 - slot, True)

    gather(step, slot, False)

    @pl.when(t == 0)
    def _():
        o_ref[...] = jnp.zeros(o_ref.shape, F32)

    u = lax.broadcasted_iota(jnp.int32, (rows, dh), 0)
    own_ok = (u <= t) & (u < t_valid)
    is_t = u == t
    raw = []
    for h in range(MOBA_N_HEADS):
        sl = slice(h * dh, (h + 1) * dh)
        q_row = jnp.sum(jnp.where(is_t, q_ref[0, :, sl], 0.0), axis=0, keepdims=True)
        q_rep = jnp.broadcast_to(q_row, (dh, dh)).astype(BF16)
        raw.append((_dot_nt(kn_ref[0, :, sl].astype(BF16), q_rep),
                    [_dot_nt(kbuf[slot, h, n].astype(BF16), q_rep) for n in range(N_SEL_PAGES)]))
    for h in range(MOBA_N_HEADS):
        sl = slice(h * dh, (h + 1) * dh)
        s_own = jnp.where(own_ok, raw[h][0] * scale, NEG_BIG)
        s_parts = [sp * scale for sp in raw[h][1]]
        m = jnp.max(s_own, axis=0, keepdims=True)
        for sp in s_parts:
            m = jnp.maximum(m, jnp.max(sp, axis=0, keepdims=True))
        p_own = jnp.exp(s_own - m)
        l = jnp.sum(p_own, axis=0, keepdims=True)
        acc = jnp.sum(p_own * vn_ref[0, :, sl], axis=0, keepdims=True)
        for n, sp in enumerate(s_parts):
            pp = jnp.exp(sp - m)
            l = l + jnp.sum(pp, axis=0, keepdims=True)
            acc = acc + jnp.sum(pp * vbuf[slot, h, n], axis=0, keepdims=True)
        o_ref[0, :, sl] = jnp.where(is_t, acc / l, o_ref[0, :, sl])


def _moba_sample_call(page_table, sel, p3, k3, v3, cache_k, cache_v, layer, *, t_valid):
    bs, rows, _ = p3.shape
    W = MOBA_WIDTH

    def new_spec(col):
        return pl.BlockSpec((1, rows, W), lambda s, pt, sl: (s // t_valid, 0, col // W))

    buf_shape = (2, MOBA_N_HEADS, N_SEL_PAGES, PAGE_SIZE, MOBA_HEAD_DIM)
    grid_spec = pltpu.PrefetchScalarGridSpec(
        num_scalar_prefetch=2,
        grid=(bs * t_valid,),
        in_specs=[new_spec(COL_Q), new_spec(0), new_spec(0),
                  pl.BlockSpec(memory_space=pl.ANY), pl.BlockSpec(memory_space=pl.ANY)],
        out_specs=pl.BlockSpec((1, rows, W), lambda s, pt, sl: (s // t_valid, 0, 0)),
        scratch_shapes=[pltpu.VMEM(buf_shape, F32), pltpu.VMEM(buf_shape, F32),
                        pltpu.SemaphoreType.DMA((2,))],
    )
    return pl.pallas_call(
        functools.partial(_moba_sample_kernel, layer=layer, t_valid=t_valid),
        grid_spec=grid_spec,
        out_shape=_sds((bs, rows, W)),
        compiler_params=_cparams(("arbitrary",)),
        name="moba_sample",
    )(page_table, sel.reshape(-1), p3, k3, v3, cache_k, cache_v)


def _merge_kernel(ys_ref, ym_ref, yl_ref, g0_ref, g1_ref, g2_ref, x_ref, gt_ref, gp_ref,
                  ws_ref, wm_ref, wl_ref, wo_ref, o_ref):
    merged = (jax.nn.sigmoid(g0_ref[...]) * _dot(ys_ref[...].astype(BF16), ws_ref[...])
              + jax.nn.sigmoid(g1_ref[...]) * _dot(ym_ref[...].astype(BF16), wm_ref[...])
              + jax.nn.sigmoid(g2_ref[...]) * _dot(yl_ref[...].astype(BF16), wl_ref[...]))
    mix = _dot(merged.astype(BF16), wo_ref[...])
    o_ref[...] = x_ref[...] + gt_ref[0] * (_rms(mix) * gp_ref[...])


def _merge_call(ys, ym, yl, p2, x2, gt, gp, ws, wm, wl, wo, *, tm, tiles_per_mod):
    rows, d = x2.shape
    mod_rows = gt.shape[1]
    row_spec = pl.BlockSpec((tm, d), lambda i: (i, 0))
    w_spec = pl.BlockSpec((d, d), lambda i: (0, 0), pipeline_mode=pl.Buffered(1))

    def gate_spec(n):
        return pl.BlockSpec((tm, d), lambda i: (i, COL_GATE // d + n))

    return pl.pallas_call(
        _merge_kernel,
        grid=(rows // tm,),
        in_specs=[row_spec, row_spec, row_spec, gate_spec(0), gate_spec(1), gate_spec(2), row_spec,
                  pl.BlockSpec((1, mod_rows, d), lambda i: (i // tiles_per_mod, 0, 0)),
                  pl.BlockSpec((1, d), lambda i: (0, 0)),
                  w_spec, w_spec, w_spec, w_spec],
        out_specs=row_spec,
        out_shape=_sds((rows, d)),
        compiler_params=_cparams(("arbitrary",)),
        name="merge",
    )(ys, ym, yl, p2, p2, p2, x2, gt, gp, ws, wm, wl, wo)


def _mlp_kernel(x_ref, sc_ref, sh_ref, gt_ref, g1_ref, g2_ref, wu_ref, wd_ref, o_ref):
    x = x_ref[...]
    h2 = (_rms(x) * g1_ref[...]) * (1.0 + sc_ref[0]) + sh_ref[0]
    up = jnp.maximum(_dot(h2.astype(BF16), wu_ref[...]), 0.0)
    f = _dot((up * up).astype(BF16), wd_ref[...])
    o_ref[...] = x + gt_ref[0] * (_rms(f) * g2_ref[...])


def _mlp_call(x2, sc, sh, gt, g1, g2, wu, wd, *, tm, tiles_per_mod):
    rows, d = x2.shape
    hid = wu.shape[1]
    mod_rows = sc.shape[1]
    row_spec = pl.BlockSpec((tm, d), lambda i: (i, 0))
    mod_spec = pl.BlockSpec((1, mod_rows, d), lambda i: (i // tiles_per_mod, 0, 0))
    vec_spec = pl.BlockSpec((1, d), lambda i: (0, 0))
    return pl.pallas_call(
        _mlp_kernel,
        grid=(rows // tm,),
        in_specs=[row_spec, mod_spec, mod_spec, mod_spec, vec_spec, vec_spec,
                  pl.BlockSpec((d, hid), lambda i: (0, 0), pipeline_mode=pl.Buffered(1)),
                  pl.BlockSpec((hid, d), lambda i: (0, 0), pipeline_mode=pl.Buffered(1))],
        out_specs=row_spec,
        out_shape=_sds((rows, d)),
        compiler_params=_cparams(("arbitrary",)),
        name="mlp",
    )(x2, sc, sh, gt, g1, g2, wu, wd)


def _block_diag_tiles(w, per_tile):
    nblk, d, _ = w.shape
    wt = w.reshape(nblk // per_tile, per_tile, d, d)
    eye = jnp.eye(per_tile, dtype=w.dtype)
    bd = wt[:, :, :, None, :] * eye[None, :, None, :, None]
    return bd.reshape(nblk // per_tile, per_tile * d, per_tile * d)


def _layer_weights(l, w_in, w_branch_ssd, w_branch_moba, w_branch_lru, w_out, w_up, w_down,
                   lru_wa, lru_wx):
    o = [0]
    for s in (SSD_D_INNER, SSD_CONV_DIM, SSD_N_HEADS, MOBA_WIDTH, MOBA_WIDTH, MOBA_WIDTH,
              LRU_WIDTH, LRU_WIDTH, N_BRANCH * D_MODEL):
        o.append(o[-1] + s)
    w = w_in[l]
    seg = lambda n: w[:, o[n]:o[n + 1]]
    w_all = jnp.concatenate([seg(1), seg(0), seg(3), seg(6), seg(7), seg(8), seg(4), seg(5)],
                            axis=1).astype(BF16)
    wdt = jnp.pad(seg(2), ((0, 0), (0, DT_PAD - SSD_N_HEADS))).astype(BF16)
    per_tile = 256 // LRU_BLOCK
    return dict(
        w_all=w_all, wdt=wdt,
        ws=w_branch_ssd[l].astype(BF16), wm=w_branch_moba[l].astype(BF16),
        wl=w_branch_lru[l].astype(BF16), wo=w_out[l].astype(BF16),
        wu=w_up[l].astype(BF16), wd=w_down[l].astype(BF16),
        wa_bd=_block_diag_tiles(lru_wa[l], per_tile).astype(BF16),
        wx_bd=_block_diag_tiles(lru_wx[l], per_tile).astype(BF16),
    )


def _row(v):
    return v.reshape(1, -1)


def _pad_lanes(v, width):
    return jnp.pad(v, (0, width - v.shape[0])).reshape(1, width)


def _halo(buf):
    return jnp.pad(buf, ((0, 0), (CONV_HALO - buf.shape[1], 0), (0, 0)))


def _group_layer(x3, mods, wts, prm, ssd_buf, ssd_h, lru_buf, lru_h, moba_fn, *, t_valid, tm_proj,
                 tm_dense, ssd_rows, lru_tm, rows_per_mod):
    bsz, t_pad, d = x3.shape
    rows = bsz * t_pad
    x2 = x3.reshape(rows, d)
    tiles_per_mod = rows_per_mod // tm_dense
    sh1, sc1, gt1, sh2, sc2, gt2 = mods

    p2, dt2, k2, v2 = _inproj_call(x2, sc1, sh1, _row(prm['g_pre_mix']), wts['w_all'], wts['wdt'],
                                   tm=tm_proj, tiles_per_mod=rows_per_mod // tm_proj)
    p3 = p2.reshape(bsz, t_pad, P_WIDTH)
    dt3 = dt2.reshape(bsz, t_pad, DT_PAD)
    k3 = k2.reshape(bsz, t_pad, MOBA_WIDTH)
    v3 = v2.reshape(bsz, t_pad, MOBA_WIDTH)

    y_ssd, ssd_h_new = _ssd_call(
        p3, dt3, _halo(ssd_buf), ssd_h.reshape(bsz, SSD_N_HEADS * SSD_HEAD_DIM, SSD_D_STATE),
        prm['ssd_conv_w'], _row(prm['ssd_conv_b']), _pad_lanes(prm['ssd_dt_bias'], DT_PAD),
        _pad_lanes(prm['ssd_a_log'], DT_PAD), _row(jnp.repeat(prm['ssd_d'], SSD_HEAD_DIM)),
        _row(prm['ssd_norm_g']), rows_in=ssd_rows, t_valid=t_valid)

    y_lru, lru_h_new = _lru_call(
        p3, _halo(lru_buf), lru_h.reshape(bsz, 1, LRU_WIDTH), prm['lru_conv_w'],
        _row(prm['lru_conv_b']), wts['wa_bd'], wts['wx_bd'], _row(prm['lru_ba']),
        _row(prm['lru_bx']), _row(prm['lru_lambda']), tm=lru_tm, t_valid=t_valid)

    y_moba = moba_fn(p3, k3, v3)

    x1 = _merge_call(y_ssd.reshape(rows, d), y_moba.reshape(rows, d), y_lru.reshape(rows, d), p2, x2,
                     gt1, _row(prm['g_post_mix']), wts['ws'], wts['wm'], wts['wl'], wts['wo'],
                     tm=tm_dense, tiles_per_mod=tiles_per_mod)
    x_out = _mlp_call(x1, sc2, sh2, gt2, _row(prm['g_pre_mlp']), _row(prm['g_post_mlp']),
                      wts['wu'], wts['wd'], tm=tm_dense, tiles_per_mod=tiles_per_mod)

    lo = t_valid - (SSD_CONV - 1)
    k_new = k3[:, :t_valid].reshape(bsz, t_valid, MOBA_N_HEADS, MOBA_HEAD_DIM)
    v_new = v3[:, :t_valid].reshape(bsz, t_valid, MOBA_N_HEADS, MOBA_HEAD_DIM)
    ssd_buf_new = p3[:, lo:t_valid, COL_XBC:COL_XBC + SSD_CONV_DIM]
    lru_buf_new = p3[:, lo:t_valid, COL_XR:COL_XR + LRU_WIDTH]
    return (x_out.reshape(bsz, t_pad, d), k_new, v_new, ssd_buf_new,
            ssd_h_new.reshape(bsz, SSD_N_HEADS, SSD_HEAD_DIM, SSD_D_STATE), lru_buf_new,
            lru_h_new.reshape(bsz, LRU_WIDTH))


def kernel(x_prompt, x_sample, c_prompt, c_sample, cache_k, cache_v, page_table, state_ssm, state_ssm_conv, state_lru, state_lru_conv, w_mod, b_mod, g_pre_mix, g_post_mix, g_pre_mlp, g_post_mlp, w_in, ssd_conv_w, ssd_conv_b, ssd_dt_bias, ssd_a_log, ssd_d, ssd_norm_g, lru_conv_w, lru_conv_b, lru_wa, lru_ba, lru_wx, lru_bx, lru_lambda, w_branch_ssd, w_branch_moba, w_branch_lru, w_out, w_up, w_down):
    bp, tp, d = x_prompt.shape
    bs, ts, _ = x_sample.shape
    depth = w_in.shape[0]
    n_pool = cache_k.shape[1]
    assert ts <= SAMPLE_T_PAD and ts >= SSD_CONV - 1
    assert (page_table.shape[1] * PAGE_SIZE) % MOBA_BLOCK == 0
    assert tp % SSD_CHUNK == 0 and tp % MOBA_BLOCK == 0

    tm_p = min(1024, tp)
    tm_dense_p = min(512, tp)
    lru_tm_p = min(256, tp)

    n_c = bp + bs
    n_c_pad = -(-n_c // SUBLANES) * SUBLANES
    c_all = jnp.pad(jnp.concatenate([c_prompt, c_sample], axis=0), ((0, n_c_pad - n_c), (0, 0)))
    mod_all = _mod_call(c_all, w_mod, b_mod)

    xs_pad = jnp.pad(x_sample, ((0, 0), (0, SAMPLE_T_PAD - ts), (0, 0)))

    zero_ssd_buf = jnp.zeros((bp, SSD_CONV - 1, SSD_CONV_DIM), F32)
    zero_ssd_h = jnp.zeros((bp, SSD_N_HEADS, SSD_HEAD_DIM, SSD_D_STATE), F32)
    zero_lru_buf = jnp.zeros((bp, LRU_CONV - 1, LRU_WIDTH), F32)
    zero_lru_h = jnp.zeros((bp, LRU_WIDTH), F32)

    outs_p = [[] for _ in range(6)]
    outs_s = [[] for _ in range(6)]
    xp, xs = x_prompt, xs_pad
    for l in range(depth):
        prm = {
            'g_pre_mix': g_pre_mix[l], 'g_post_mix': g_post_mix[l], 'g_pre_mlp': g_pre_mlp[l],
            'g_post_mlp': g_post_mlp[l], 'ssd_conv_w': ssd_conv_w[l], 'ssd_conv_b': ssd_conv_b[l],
            'ssd_dt_bias': ssd_dt_bias[l], 'ssd_a_log': ssd_a_log[l], 'ssd_d': ssd_d[l],
            'ssd_norm_g': ssd_norm_g[l], 'lru_conv_w': lru_conv_w[l], 'lru_conv_b': lru_conv_b[l],
            'lru_ba': lru_ba[l].reshape(-1), 'lru_bx': lru_bx[l].reshape(-1),
            'lru_lambda': lru_lambda[l],
        }
        wts = _layer_weights(l, w_in, w_branch_ssd, w_branch_moba, w_branch_lru, w_out, w_up,
                             w_down, lru_wa, lru_wx)
        mod_l = mod_all[l]
        mods_p = [mod_l[:bp, n * d:(n + 1) * d].reshape(bp, 1, d) for n in range(6)]
        mods_s = [jnp.repeat(mod_l[bp:bp + bs, n * d:(n + 1) * d], SAMPLE_T_PAD, axis=0)
                  .reshape(1, bs * SAMPLE_T_PAD, d) for n in range(6)]

        ksum_box = []

        def moba_prompt(p3, k3, v3, l=l, ksum_box=ksum_box):
            y, ksum = _moba_prompt_call(p3, k3, v3, page_table, cache_k, l)
            ksum_box.append(ksum)
            return y

        res_p = _group_layer(xp, mods_p, wts, prm, zero_ssd_buf, zero_ssd_h, zero_lru_buf, zero_lru_h,
                             moba_prompt, t_valid=tp, tm_proj=tm_p, tm_dense=tm_dense_p,
                             ssd_rows=SSD_CHUNK, lru_tm=lru_tm_p, rows_per_mod=tp)
        xp = res_p[0]
        for n in range(6):
            outs_p[n].append(res_p[n + 1])

        def moba_sample(p3, k3, v3, l=l, ksum_box=ksum_box):
            sel = _select_call(p3, ksum_box[0])
            sel = sel[:, :ts, :MOBA_N_HEADS * 4].reshape(bs, ts, MOBA_N_HEADS, 4)[..., :MOBA_TOPK]
            return _moba_sample_call(page_table, sel, p3, k3, v3, cache_k, cache_v, l, t_valid=ts)

        res_s = _group_layer(xs, mods_s, wts, prm, state_ssm_conv[l], state_ssm[l], state_lru_conv[l],
                             state_lru[l], moba_sample, t_valid=ts, tm_proj=bs * SAMPLE_T_PAD,
                             tm_dense=bs * SAMPLE_T_PAD, ssd_rows=SAMPLE_T_PAD, lru_tm=SAMPLE_T_PAD,
                             rows_per_mod=bs * SAMPLE_T_PAD)
        xs = res_s[0]
        for n in range(6):
            outs_s[n].append(res_s[n + 1])

    st = lambda lst: jnp.stack(lst)
    kp, vp, ssmc_p, ssm_p, lruc_p, lru_p = outs_p
    ks, vs, ssmc_s, ssm_s, lruc_s, lru_s = outs_s
    return (xp, xs[:, :ts],
            st(kp), st(vp), st(ks), st(vs),
            st(ssm_p), st(ssmc_p), st(lru_p), st(lruc_p),
            st(ssm_s), st(ssmc_s), st(lru_s), st(lruc_s))
```

```python
import functools
import math

import jax
import jax.numpy as jnp
from jax import lax
from jax.experimental import pallas as pl
from jax.experimental.pallas import tpu as pltpu

F32 = jnp.float32
BF16 = jnp.bfloat16
HIGHEST = lax.Precision.HIGHEST

D_MODEL = 1024
PAGE_SIZE = 128
SSD_D_INNER = D_MODEL
SSD_HEAD_DIM = 64
SSD_N_HEADS = SSD_D_INNER // SSD_HEAD_DIM
SSD_N_GROUPS = 4
SSD_D_STATE = 128
SSD_CONV = 4
SSD_CHUNK = 128
SSD_CONV_DIM = SSD_D_INNER + 2 * SSD_N_GROUPS * SSD_D_STATE
MOBA_N_HEADS = 8
MOBA_HEAD_DIM = 128
MOBA_WIDTH = MOBA_N_HEADS * MOBA_HEAD_DIM
MOBA_BLOCK = 256
MOBA_TOPK = 3
LRU_WIDTH = D_MODEL
LRU_N_BLOCKS = 16
LRU_BLOCK = LRU_WIDTH // LRU_N_BLOCKS
LRU_CONV = 4
LRU_C = 8.0
N_BRANCH = 3
MLP_HIDDEN = 4 * D_MODEL
RMS_EPS = 1e-6

SUBLANES = 8
LANES = 128
VMEM_LIMIT_BYTES = 56 * 1024 * 1024

SAMPLE_T_PAD = SUBLANES
CONV_HALO = SUBLANES
NEG_BIG = -1e30
LOG2E = math.log2(math.e)

COL_XBC = 0
COL_Z = COL_XBC + SSD_CONV_DIM
COL_Q = COL_Z + SSD_D_INNER
COL_XR = COL_Q + MOBA_WIDTH
COL_GR = COL_XR + LRU_WIDTH
COL_GATE = COL_GR + LRU_WIDTH
P_WIDTH = COL_GATE + N_BRANCH * D_MODEL
PROJ_TN = 1024
P_TILES = P_WIDTH // PROJ_TN
DT_PAD = LANES


def _cparams(sem):
    return pltpu.CompilerParams(dimension_semantics=sem, vmem_limit_bytes=VMEM_LIMIT_BYTES)


def _sds(shape, dtype=F32):
    return jax.ShapeDtypeStruct(shape, dtype)


def _silu(x):
    return x * jax.nn.sigmoid(x)


def _softplus(x):
    return jnp.maximum(x, 0.0) + jnp.log1p(jnp.exp(-jnp.abs(x)))


def _rms(x):
    return x * lax.rsqrt(jnp.mean(x * x, axis=-1, keepdims=True) + RMS_EPS)


def _dot(a, b):
    return jnp.dot(a, b, preferred_element_type=F32)


def _dot_nt(a, b, precision=None):
    return lax.dot_general(a, b, (((1,), (1,)), ((), ())), precision=precision,
                           preferred_element_type=F32)


def _mod_kernel(c_ref, w_ref, b_ref, o_ref):
    s = _silu(c_ref[...]).astype(BF16)
    o_ref[0] = _dot(s, w_ref[0].astype(BF16)) + b_ref[0]


def _mod_call(c_all, w_mod, b_mod):
    depth, d, n = w_mod.shape
    rows = c_all.shape[0]
    tn = 1536
    return pl.pallas_call(
        _mod_kernel,
        grid=(depth, n // tn),
        in_specs=[pl.BlockSpec((rows, d), lambda l, j: (0, 0)),
                  pl.BlockSpec((1, d, tn), lambda l, j: (l, 0, j)),
                  pl.BlockSpec((1, 1, tn), lambda l, j: (l, 0, j))],
        out_specs=pl.BlockSpec((1, rows, tn), lambda l, j: (l, 0, j)),
        out_shape=_sds((depth, rows, n)),
        compiler_params=_cparams(("arbitrary", "arbitrary")),
        name="mod",
    )(c_all, w_mod, b_mod.reshape(depth, 1, n))


def _inproj_kernel(x_ref, sc_ref, sh_ref, g_ref, w_ref, wdt_ref, p_ref, dt_ref, k_ref, v_ref, h_scr):
    j = pl.program_id(1)

    @pl.when(j == 0)
    def _():
        y = _rms(x_ref[...]) * g_ref[...]
        hb = (y * (1.0 + sc_ref[0]) + sh_ref[0]).astype(BF16)
        h_scr[...] = hb
        dt_ref[...] = _dot(hb, wdt_ref[...])

    @pl.when(j < P_TILES)
    def _():
        p_ref[...] = _dot(h_scr[...], w_ref[...])

    @pl.when(j == P_TILES)
    def _():
        k_ref[...] = _dot(h_scr[...], w_ref[...])

    @pl.when(j == P_TILES + 1)
    def _():
        v_ref[...] = _dot(h_scr[...], w_ref[...])


MOD_SH1, MOD_SC1, MOD_GT1, MOD_SH2, MOD_SC2, MOD_GT2 = range(6)


def _mod_spec(mod, tiles_per_mod, which, d, n_grid):
    shape = (1, mod.shape[1], d)
    if n_grid == 2:
        return pl.BlockSpec(shape, lambda i, j: (i // tiles_per_mod, 0, which))
    return pl.BlockSpec(shape, lambda i: (i // tiles_per_mod, 0, which))


def _inproj_call(x2, mod, g, w_all, wdt, *, tm, tiles_per_mod):
    rows, d = x2.shape
    assert PROJ_TN == MOBA_WIDTH and w_all.shape[1] == P_WIDTH + 2 * MOBA_WIDTH
    kv_spec = pl.BlockSpec((tm, MOBA_WIDTH), lambda i, j: (i, 0))
    return pl.pallas_call(
        _inproj_kernel,
        grid=(rows // tm, P_TILES + 2),
        in_specs=[pl.BlockSpec((tm, d), lambda i, j: (i, 0)),
                  _mod_spec(mod, tiles_per_mod, MOD_SC1, d, 2),
                  _mod_spec(mod, tiles_per_mod, MOD_SH1, d, 2),
                  pl.BlockSpec((1, d), lambda i, j: (0, 0)),
                  pl.BlockSpec((d, PROJ_TN), lambda i, j: (0, j)),
                  pl.BlockSpec((d, DT_PAD), lambda i, j: (0, 0))],
        out_specs=[pl.BlockSpec((tm, PROJ_TN), lambda i, j: (i, jnp.minimum(j, P_TILES - 1))),
                   pl.BlockSpec((tm, DT_PAD), lambda i, j: (i, 0)),
                   kv_spec, kv_spec],
        out_shape=[_sds((rows, P_WIDTH)), _sds((rows, DT_PAD)),
                   _sds((rows, MOBA_WIDTH)), _sds((rows, MOBA_WIDTH))],
        scratch_shapes=[pltpu.VMEM((tm, d), BF16)],
        compiler_params=_cparams(("arbitrary", "arbitrary")),
        name="inproj",
    )(x2, mod, mod, g, w_all, wdt)


def _conv_tile(prev_scr, x, buf_ref, w_ref, b_ref, tile_idx):
    rows, width = x.shape

    @pl.when(tile_idx == 0)
    def _():
        prev_scr[...] = buf_ref[0]

    prev = prev_scr[...]
    prev_scr[...] = x[rows - CONV_HALO:, :]
    n_groups = rows // CONV_HALO
    x3 = jnp.concatenate([prev, x], axis=0).reshape(n_groups + 1, CONV_HALO, width)
    row = lax.broadcasted_iota(jnp.int32, (1, CONV_HALO, width), 1)
    acc = b_ref[...] + w_ref[SSD_CONV - 1:SSD_CONV, :] * x
    for k in range(1, SSD_CONV):
        r3 = pltpu.roll(x3, k, 1)
        shifted = jnp.where(row >= k, r3[1:], r3[:-1]).reshape(rows, width)
        acc = acc + w_ref[SSD_CONV - 1 - k:SSD_CONV - k, :] * shifted
    return acc


def _ssd_kernel(xbc_ref, z_ref, dt_ref, buf_ref, h0_ref, cw_ref, cb_ref, dtb_ref, alog_ref,
                dsk_ref, ng_ref, y_ref, h_ref, xp_scr, dt_scr, *, rows_in, t_valid, n_seq):
    L = SSD_CHUNK
    N = SSD_D_STATE
    P2 = 2 * SSD_HEAD_DIM
    c = pl.program_id(1)

    @pl.when(c == 0)
    def _():
        h_ref[...] = h0_ref[...]

    lane = lax.broadcasted_iota(jnp.int32, (L, LANES), 1)
    row = lax.broadcasted_iota(jnp.int32, (L, LANES), 0)
    ri = lax.broadcasted_iota(jnp.int32, (L, L), 0)
    ci = lax.broadcasted_iota(jnp.int32, (L, L), 1)
    causal = ri >= ci
    left = lax.broadcasted_iota(jnp.int32, (L, P2), 1) < SSD_HEAD_DIM
    left_rows = lax.broadcasted_iota(jnp.int32, (P2, N), 0) < SSD_HEAD_DIM
    a = jnp.where(lane < SSD_N_HEADS, -jnp.exp(alog_ref[...]), 0.0)

    ctx = []
    for s in range(n_seq):
        xc = _silu(_conv_tile(xp_scr.at[s], xbc_ref[s], buf_ref.at[pl.ds(s, 1)], cw_ref, cb_ref, c))
        if rows_in < L:
            xc = jnp.concatenate([xc, jnp.zeros((L - rows_in, SSD_CONV_DIM), F32)], axis=0)
            dt_scr[s] = jnp.zeros((L, LANES), F32)
        dt_scr[s, 0:rows_in, :] = dt_ref[s]
        dtv = _softplus(dt_scr[s] + dtb_ref[...])
        dtv = jnp.where((row + c * L < t_valid) & (lane < SSD_N_HEADS), dtv, 0.0)
        cum = jnp.dot(causal.astype(F32), dtv * a, precision=HIGHEST, preferred_element_type=F32)
        cum_t = cum.T
        cum_last = cum[L - 1:L, :]
        ctx.append(dict(
            xs=xc[:, :SSD_D_INNER],
            bm=xc[:, SSD_D_INNER:SSD_D_INNER + SSD_N_GROUPS * N],
            cm=xc[:, SSD_D_INNER + SSD_N_GROUPS * N:],
            cum=cum, cum_t=cum_t, dtv_t=dtv.T, ecum=jnp.exp(cum),
            wgt=jnp.exp(cum_last - cum) * dtv,
            cdec_t=jnp.exp(cum_t[:, L - 1:L]),
            y_tiles=[]))

    for j in range(SSD_N_HEADS // 2):
        g = j // (SSD_N_HEADS // SSD_N_GROUPS // 2)
        h_a, h_b = 2 * j, 2 * j + 1
        for s in range(n_seq):
            k = ctx[s]
            bm_g = k['bm'][:, g * N:(g + 1) * N]
            cm_g = k['cm'][:, g * N:(g + 1) * N].astype(BF16)
            cb = _dot_nt(cm_g, bm_g.astype(BF16))
            xs_p = k['xs'][:, j * P2:(j + 1) * P2]
            m_parts = []
            for hh in (h_a, h_b):
                seg = k['cum'][:, hh:hh + 1] - k['cum_t'][hh:hh + 1, :]
                dec = jnp.exp(jnp.where(causal, seg, -jnp.inf))
                m_parts.append((cb * dec * k['dtv_t'][hh:hh + 1, :]).astype(BF16))
            m_pair = jnp.concatenate(m_parts, axis=1)
            x_bd = jnp.concatenate([jnp.where(left, xs_p, 0.0), jnp.where(left, 0.0, xs_p)],
                                   axis=0).astype(BF16)
            y_diag = _dot(m_pair, x_bd)
            h_prev = h_ref[s, j * P2:(j + 1) * P2, :]
            y_off = _dot_nt(cm_g, h_prev.astype(BF16))
            e_sel = jnp.where(left, k['ecum'][:, h_a:h_a + 1], k['ecum'][:, h_b:h_b + 1])
            k['y_tiles'].append(y_diag + y_off * e_sel + dsk_ref[:, j * P2:(j + 1) * P2] * xs_p)
            w_sel = jnp.where(left, k['wgt'][:, h_a:h_a + 1], k['wgt'][:, h_b:h_b + 1])
            xw_t = (xs_p * w_sel).T.astype(BF16)
            st = _dot(xw_t, bm_g.astype(BF16))
            d_sel = jnp.where(left_rows, k['cdec_t'][h_a:h_a + 1, :], k['cdec_t'][h_b:h_b + 1, :])
            h_ref[s, j * P2:(j + 1) * P2, :] = h_prev * d_sel + st

    gw = SSD_D_INNER // SSD_N_GROUPS
    tiles_per_group = gw // P2
    for s in range(n_seq):
        zs = _silu(z_ref[s])
        for g in range(SSD_N_GROUPS):
            yg = jnp.concatenate(ctx[s]['y_tiles'][g * tiles_per_group:(g + 1) * tiles_per_group], axis=1)
            yg = yg[0:rows_in, :] * zs[:, g * gw:(g + 1) * gw]
            y_ref[s, :, g * gw:(g + 1) * gw] = (_rms(yg) * ng_ref[:, g * gw:(g + 1) * gw]).astype(BF16)


def _ssd_call(p3, dt3, buf8, h0, h0_layer, cw, cb, dtb, alog, dsk, ng, *, rows_in, t_valid, seqs):
    bsz, t_pad, _ = p3.shape
    nc = t_pad // rows_in
    L = SSD_CHUNK
    hp = SSD_N_HEADS * SSD_HEAD_DIM
    ns = math.gcd(seqs, bsz)
    const2 = lambda b, c: (0, 0)
    kern = functools.partial(_ssd_kernel, rows_in=rows_in, t_valid=t_valid, n_seq=ns)
    return pl.pallas_call(
        kern,
        grid=(bsz // ns, nc),
        in_specs=[pl.BlockSpec((ns, rows_in, SSD_CONV_DIM), lambda b, c: (b, c, COL_XBC // SSD_CONV_DIM)),
                  pl.BlockSpec((ns, rows_in, SSD_D_INNER), lambda b, c: (b, c, COL_Z // SSD_D_INNER)),
                  pl.BlockSpec((ns, rows_in, DT_PAD), lambda b, c: (b, c, 0)),
                  pl.BlockSpec((ns, CONV_HALO, SSD_CONV_DIM), lambda b, c: (b, 0, 0)),
                  pl.BlockSpec((None, ns, hp, SSD_D_STATE), lambda b, c: (h0_layer, b, 0, 0)),
                  pl.BlockSpec((SSD_CONV, SSD_CONV_DIM), const2),
                  pl.BlockSpec((1, SSD_CONV_DIM), const2),
                  pl.BlockSpec((1, DT_PAD), const2),
                  pl.BlockSpec((1, DT_PAD), const2),
                  pl.BlockSpec((1, SSD_D_INNER), const2),
                  pl.BlockSpec((1, SSD_D_INNER), const2)],
        out_specs=[pl.BlockSpec((ns, rows_in, SSD_D_INNER), lambda b, c: (b, c, 0)),
                   pl.BlockSpec((ns, hp, SSD_D_STATE), lambda b, c: (b, 0, 0))],
        out_shape=[_sds((bsz, t_pad, SSD_D_INNER), BF16), _sds((bsz, hp, SSD_D_STATE))],
        scratch_shapes=[pltpu.VMEM((ns, CONV_HALO, SSD_CONV_DIM), F32),
                        pltpu.VMEM((ns, L, DT_PAD), F32)],
        compiler_params=_cparams(("arbitrary", "arbitrary")),
        name="ssd",
    )(p3, p3, dt3, buf8, h0, cw, cb, dtb, alog, dsk, ng)


def _lru_kernel(xr_ref, gr_ref, buf_ref, h0_ref, cw_ref, cb_ref, wa_ref, wx_ref, ba_ref, bx_ref,
                lam_ref, y_ref, h_ref, xp_scr, *, tm, t_valid, t_pad):
    i = pl.program_id(1)

    @pl.when(i == 0)
    def _():
        h_ref[0] = h0_ref[0]

    xc = _conv_tile(xp_scr, xr_ref[0], buf_ref, cw_ref, cb_ref, i)
    bw = wa_ref.shape[1]
    r_parts, i_parts = [], []
    for j in range(LRU_WIDTH // bw):
        xj = xc[:, j * bw:(j + 1) * bw].astype(BF16)
        r_parts.append(_dot(xj, wa_ref[j]))
        i_parts.append(_dot(xj, wx_ref[j]))
    r = jax.nn.sigmoid(jnp.concatenate(r_parts, axis=1) + ba_ref[...])
    ig = jax.nn.sigmoid(jnp.concatenate(i_parts, axis=1) + bx_ref[...])
    log_a = (-LRU_C) * r * _softplus(-lam_ref[...])
    a_full = jnp.exp(log_a)
    b_full = jnp.sqrt(-jnp.tanh(log_a) * (a_full * a_full + 1.0)) * (ig * xc)
    gate = jax.nn.gelu(gr_ref[0], approximate=True)

    n_groups = tm // SUBLANES
    row = lax.broadcasted_iota(jnp.int32, (tm, LANES), 0)
    row_in_group = lax.broadcasted_iota(jnp.int32, (1, SUBLANES, LANES), 1)
    for s in range(LRU_WIDTH // LANES):
        sl = slice(s * LANES, (s + 1) * LANES)
        a, b = a_full[:, sl], b_full[:, sl]
        if t_valid < t_pad:
            valid = row + i * tm < t_valid
            a = jnp.where(valid, a, 1.0)
            b = jnp.where(valid, b, 0.0)
        a = a.reshape(n_groups, SUBLANES, LANES)
        b = b.reshape(n_groups, SUBLANES, LANES)
        k = 1
        while k < SUBLANES:
            keep = row_in_group >= k
            b = b + a * jnp.where(keep, pltpu.roll(b, k, 1), 0.0)
            a = a * jnp.where(keep, pltpu.roll(a, k, 1), 1.0)
            k *= 2
        carry = h_ref[0, :, sl]
        groups = []
        for g in range(n_groups):
            hg = b[g] + a[g] * carry
            groups.append(hg)
            carry = hg[SUBLANES - 1:SUBLANES, :]
        h = groups[0] if n_groups == 1 else jnp.concatenate(groups, axis=0)
        y_ref[0, :, sl] = (h * gate[:, sl]).astype(BF16)
        h_ref[0, :, sl] = carry


def _lru_call(p3, buf8, h0, cw, cb, wa_bd, wx_bd, ba, bx, lam, *, tm, t_valid):
    bsz, t_pad, _ = p3.shape
    W = LRU_WIDTH
    nb, bw, _ = wa_bd.shape
    const2 = lambda b, i: (0, 0)
    const3 = lambda b, i: (0, 0, 0)
    kern = functools.partial(_lru_kernel, tm=tm, t_valid=t_valid, t_pad=t_pad)
    return pl.pallas_call(
        kern,
        grid=(bsz, t_pad // tm),
        in_specs=[pl.BlockSpec((1, tm, W), lambda b, i: (b, i, COL_XR // W)),
                  pl.BlockSpec((1, tm, W), lambda b, i: (b, i, COL_GR // W)),
                  pl.BlockSpec((1, CONV_HALO, W), lambda b, i: (b, 0, 0)),
                  pl.BlockSpec((1, 1, W), lambda b, i: (b, 0, 0)),
                  pl.BlockSpec((LRU_CONV, W), const2),
                  pl.BlockSpec((1, W), const2),
                  pl.BlockSpec((nb, bw, bw), const3),
                  pl.BlockSpec((nb, bw, bw), const3),
                  pl.BlockSpec((1, W), const2),
                  pl.BlockSpec((1, W), const2),
                  pl.BlockSpec((1, W), const2)],
        out_specs=[pl.BlockSpec((1, tm, W), lambda b, i: (b, i, 0)),
                   pl.BlockSpec((1, 1, W), lambda b, i: (b, 0, 0))],
        out_shape=[_sds((bsz, t_pad, W), BF16), _sds((bsz, 1, W))],
        scratch_shapes=[pltpu.VMEM((CONV_HALO, W), F32)],
        compiler_params=_cparams(("arbitrary", "arbitrary")),
        name="lru",
    )(p3, p3, buf8, h0, cw, cb, wa_bd, wx_bd, ba, bx, lam)


def _topk_mask(gate, n_valid, col):
    nb = gate.shape[1]
    past = col < n_valid
    gm = jnp.where(past, gate, -jnp.inf)
    rank = jnp.zeros(gate.shape, F32)
    for j in range(nb):
        gj = gm[:, j:j + 1]
        ahead = (gj > gm) | ((gj == gm) & (col > j))
        rank = rank + jnp.where(ahead, 1.0, 0.0)
    return (rank < MOBA_TOPK) & past, rank


PAGES_PER_BLOCK = MOBA_BLOCK // PAGE_SIZE


def _ksum_stream(pt_ref, ck_hbm, ks_ref, kbuf, sem, step, n_steps, *, layer, pps, n_pages):
    def copies(st, slot, lookup):
        out = []
        for u in range(pps):
            if lookup:
                p = st * pps + u
                page = pt_ref[lax.div(p, n_pages), lax.rem(p, n_pages)]
            else:
                page = 0
            out.append(pltpu.make_async_copy(ck_hbm.at[layer, page], kbuf.at[slot, u], sem.at[slot]))
        return out

    slot = lax.rem(step, 2)

    @pl.when(step == 0)
    def _():
        for c in copies(step, slot, True):
            c.start()

    @pl.when(step + 1 < n_steps)
    def _():
        for c in copies(step + 1, 1 - slot, True):
            c.start()

    for c in copies(step, slot, False):
        c.wait()

    blk0 = lax.div(lax.rem(step * pps, n_pages), PAGES_PER_BLOCK)

    @pl.when(blk0 == 0)
    def _():
        ks_ref[...] = jnp.zeros(ks_ref.shape, F32)

    sums = []
    for r in range(pps // PAGES_PER_BLOCK):
        acc = jnp.sum(kbuf[slot, r * PAGES_PER_BLOCK], axis=0)
        for u in range(1, PAGES_PER_BLOCK):
            acc = acc + jnp.sum(kbuf[slot, r * PAGES_PER_BLOCK + u], axis=0)
        sums.append(acc)
    row = lax.broadcasted_iota(jnp.int32, ks_ref.shape[2:], 0)
    for h in range(MOBA_N_HEADS):
        cur = ks_ref[0, h]
        for r, acc in enumerate(sums):
            cur = jnp.where(row == blk0 + r, acc[h:h + 1, :], cur)
        ks_ref[0, h] = cur


def _moba_prompt_kernel(pt_ref, q_ref, k_ref, v_ref, ck_hbm, o_ref, ks_ref, kmean_scr, kg_scr, vtg_scr,
                        sel_scr, kbuf, sem, *, nb, grp, nh, layer, pps, n_pages):
    i = pl.program_id(2)
    BS = MOBA_BLOCK
    GB = grp * BS
    dh = MOBA_HEAD_DIM
    scale = MOBA_HEAD_DIM ** -0.5

    step = (pl.program_id(0) * pl.num_programs(1) + pl.program_id(1)) * pl.num_programs(2) + i
    n_steps = pl.num_programs(0) * pl.num_programs(1) * pl.num_programs(2)
    _ksum_stream(pt_ref, ck_hbm, ks_ref, kbuf, sem, step, n_steps, layer=layer, pps=pps,
                 n_pages=n_pages)

    @pl.when(i == 0)
    def _():
        for hh in range(nh):
            hs = slice(hh * dh, (hh + 1) * dh)
            for j in range(nb):
                kmean_scr[hh, j:j + 1, :] = jnp.mean(k_ref[0, j * BS:(j + 1) * BS, hs], axis=0,
                                                     keepdims=True)
            for jj in range(nb // grp):
                kg_scr[hh, jj] = k_ref[0, jj * GB:(jj + 1) * GB, hs].astype(BF16)
                for u in range(grp):
                    r0 = jj * GB + u * BS
                    vtg_scr[hh, jj, :, u * BS:(u + 1) * BS] = v_ref[0, r0:r0 + BS, hs].T.astype(BF16)

    qb_t = []
    for hh in range(nh):
        q_t = q_ref[0, :, hh * dh:(hh + 1) * dh].T
        qb_t.append(q_t.astype(BF16))
        gate = jnp.dot(kmean_scr[hh], q_t, precision=HIGHEST, preferred_element_type=F32)
        blk = lax.broadcasted_iota(jnp.int32, gate.shape, 0)
        past = blk < i
        gm = jnp.where(past, gate, -jnp.inf)
        rank = jnp.zeros(gate.shape, F32)
        for j in range(nb):
            gj = gm[j:j + 1, :]
            ahead = (gj > gm) | ((gj == gm) & (blk > j))
            rank = rank + jnp.where(ahead, 1.0, 0.0)
        sel = jnp.where((rank < MOBA_TOPK) & past, 0.0, -jnp.inf)
        for j in range(nb):
            sel_scr[hh, j] = jnp.broadcast_to(sel[j:j + 1, :], (SUBLANES, BS))

    def block_bias(hh, j):
        rows = sel_scr[hh, j]
        return jnp.broadcast_to(rows[None], (BS // SUBLANES, SUBLANES, BS)).reshape(BS, BS)

    sub = min(MOBA_SUB, grp)
    n_parts = grp // sub
    SB = sub * BS

    def scores(hh, jj, part):
        return _dot(kg_scr[hh, jj, part * SB:(part + 1) * SB, :], qb_t[hh])

    def update(carry, hh, jj, part, s_raw, bias):
        m, l, acc = carry
        s = s_raw * (scale * LOG2E) + bias
        m_new = jnp.maximum(m, jnp.max(s, axis=0, keepdims=True))
        alpha = jnp.exp2(m - m_new)
        p = jnp.exp2(s - m_new)
        l_new = alpha * l + jnp.sum(p, axis=0, keepdims=True)
        v_t = vtg_scr[hh, jj, :, part * SB:(part + 1) * SB]
        acc_new = alpha * acc + _dot(v_t, p.astype(BF16))
        return m_new, l_new, acc_new

    def run_parts(jj, carries, bias_fn, parts):
        s_raw = {part: [scores(hh, jj, part) for hh in range(nh)] for part in parts}
        carries = list(carries)
        for part in parts:
            for hh in range(nh):
                bias = jnp.concatenate(
                    [bias_fn(hh, jj * grp + part * sub + u) for u in range(sub)], axis=0)
                carries[hh] = update(carries[hh], hh, jj, part, s_raw[part][hh], bias)
        return tuple(carries)

    def body(jj, carries):
        return run_parts(jj, carries, block_bias, range(n_parts))

    init = tuple((jnp.full((1, BS), NEG_BIG, F32), jnp.zeros((1, BS), F32), jnp.zeros((dh, BS), F32))
                 for _ in range(nh))
    last = i // grp
    carries = lax.fori_loop(0, last, body, init)

    key = lax.broadcasted_iota(jnp.int32, (BS, BS), 0)
    qry = lax.broadcasted_iota(jnp.int32, (BS, BS), 1)
    causal_bias = jnp.where(key <= qry, 0.0, -jnp.inf)

    def last_bias(hh, j):
        return jnp.where(j == i, causal_bias, block_bias(hh, j))

    carries = run_parts(last, carries, last_bias, [0])
    for part in range(1, n_parts):
        carries = lax.cond(last * grp + part * sub <= i,
                           lambda c, part=part: run_parts(last, c, last_bias, [part]),
                           lambda c: c, carries)
    for hh in range(nh):
        m, l, acc = carries[hh]
        o_ref[0, :, hh * dh:(hh + 1) * dh] = (acc / l).T.astype(BF16)


MOBA_GROUP = 4
MOBA_SUB = 2
MOBA_HEADS_PER_STEP = 4


def _moba_prompt_call(p3, k3, v3, page_table, cache_k, layer):
    bsz, t, _ = p3.shape
    assert t % MOBA_BLOCK == 0 and MOBA_N_HEADS % MOBA_HEADS_PER_STEP == 0
    nb = t // MOBA_BLOCK
    grp = math.gcd(MOBA_GROUP, nb)
    nh = MOBA_HEADS_PER_STEP
    dh = MOBA_HEAD_DIM
    hw = nh * dh
    n_hg = MOBA_N_HEADS // nh
    bs, n_pages = page_table.shape
    n_steps = bsz * n_hg * nb
    pps = (bs * n_pages) // n_steps
    assert pps * n_steps == bs * n_pages and pps % PAGES_PER_BLOCK == 0 and n_pages % pps == 0
    nbs = n_pages // PAGES_PER_BLOCK

    def seq_of(b, h, i):
        return (((b * n_hg + h) * nb + i) * pps) // n_pages

    kern = functools.partial(_moba_prompt_kernel, nb=nb, grp=grp, nh=nh, layer=layer, pps=pps,
                             n_pages=n_pages)
    grid_spec = pltpu.PrefetchScalarGridSpec(
        num_scalar_prefetch=1,
        grid=(bsz, n_hg, nb),
        in_specs=[pl.BlockSpec((1, MOBA_BLOCK, hw), lambda b, h, i, pt: (b, i, COL_Q // hw + h)),
                  pl.BlockSpec((1, t, hw), lambda b, h, i, pt: (b, 0, h), pipeline_mode=pl.Buffered(1)),
                  pl.BlockSpec((1, t, hw), lambda b, h, i, pt: (b, 0, h), pipeline_mode=pl.Buffered(1)),
                  pl.BlockSpec(memory_space=pl.ANY)],
        out_specs=[pl.BlockSpec((1, MOBA_BLOCK, hw), lambda b, h, i, pt: (b, i, h)),
                   pl.BlockSpec((1, MOBA_N_HEADS, nbs, dh), lambda b, h, i, pt: (seq_of(b, h, i), 0, 0, 0))],
        scratch_shapes=[pltpu.VMEM((nh, nb, dh), F32),
                        pltpu.VMEM((nh, nb // grp, grp * MOBA_BLOCK, dh), BF16),
                        pltpu.VMEM((nh, nb // grp, dh, grp * MOBA_BLOCK), BF16),
                        pltpu.VMEM((nh, nb, SUBLANES, MOBA_BLOCK), F32),
                        pltpu.VMEM((2, pps, PAGE_SIZE, MOBA_N_HEADS, dh), F32),
                        pltpu.SemaphoreType.DMA((2,))],
    )
    return pl.pallas_call(
        kern,
        grid_spec=grid_spec,
        out_shape=[_sds((bsz, t, MOBA_WIDTH), BF16), _sds((bs, MOBA_N_HEADS, nbs, dh))],
        compiler_params=_cparams(("arbitrary", "arbitrary", "arbitrary")),
        name="moba_prompt",
    )(page_table, p3, k3, v3, cache_k)


def _select_kernel(q_ref, ks_ref, o_ref, *, nb):
    rows = q_ref.shape[1]
    col = lax.broadcasted_iota(jnp.int32, (rows, nb), 1)
    colf = col.astype(F32)
    lane = lax.broadcasted_iota(jnp.int32, (rows, LANES), 1)
    out = jnp.zeros((rows, LANES), F32)
    for h in range(MOBA_N_HEADS):
        sl = slice(h * MOBA_HEAD_DIM, (h + 1) * MOBA_HEAD_DIM)
        kmean = ks_ref[0, h] * (1.0 / MOBA_BLOCK)
        gate = _dot_nt(q_ref[0, :, sl], kmean, precision=HIGHEST)
        sel, rank = _topk_mask(gate, nb, col)
        for r in range(MOBA_TOPK):
            idx = jnp.sum(jnp.where(sel & (rank == r), colf, 0.0), axis=1, keepdims=True)
            out = jnp.where(lane == h * 4 + r, idx, out)
    o_ref[0] = out.astype(jnp.int32)


def _select_call(p3, ksum):
    bs, rows, _ = p3.shape
    nb = ksum.shape[2]
    assert nb >= MOBA_TOPK
    return pl.pallas_call(
        functools.partial(_select_kernel, nb=nb),
        grid=(bs,),
        in_specs=[pl.BlockSpec((1, rows, MOBA_WIDTH), lambda b: (b, 0, COL_Q // MOBA_WIDTH)),
                  pl.BlockSpec((1, MOBA_N_HEADS, nb, MOBA_HEAD_DIM), lambda b: (b, 0, 0, 0))],
        out_specs=pl.BlockSpec((1, rows, LANES), lambda b: (b, 0, 0)),
        out_shape=_sds((bs, rows, LANES), jnp.int32),
        compiler_params=_cparams(("arbitrary",)),
        name="moba_select",
    )(p3, ksum)


N_SEL_PAGES = MOBA_TOPK * PAGES_PER_BLOCK


def _moba_sample_kernel(pt_ref, sel_ref, q_ref, kn_ref, vn_ref, ck_hbm, cv_hbm, o_ref,
                        kbuf, vbuf, sem, *, layer, t_valid):
    step = pl.program_id(0)
    n_steps = pl.num_programs(0)
    t = step % t_valid
    slot = step % 2
    rows = q_ref.shape[1]
    dh = MOBA_HEAD_DIM
    scale = MOBA_HEAD_DIM ** -0.5

    def gather(st, sl, start):
        b = st // t_valid
        for h in range(MOBA_N_HEADS):
            for r in range(MOBA_TOPK):
                blk = sel_ref[(st * MOBA_N_HEADS + h) * MOBA_TOPK + r] if start else 0
                for half in range(PAGES_PER_BLOCK):
                    page = pt_ref[b, blk * PAGES_PER_BLOCK + half] if start else 0
                    n = r * PAGES_PER_BLOCK + half
                    for src, dst in ((ck_hbm, kbuf), (cv_hbm, vbuf)):
                        cp = pltpu.make_async_copy(src.at[layer, page, :, h, :], dst.at[sl, h, n],
                                                   sem.at[sl])
                        if start:
                            cp.start()
                        else:
                            cp.wait()

    @pl.when(step == 0)
    def _():
        gather(step, slot, True)

    @pl.when(step + 1 < n_steps)
    def _():
        gather(step + 1, 1 - slot, True)

    gather(step, slot, False)

    @pl.when(t == 0)
    def _():
        o_ref[...] = jnp.zeros(o_ref.shape, F32)

    u = lax.broadcasted_iota(jnp.int32, (rows, dh), 0)
    own_ok = (u <= t) & (u < t_valid)
    is_t = u == t
    raw = []
    for h in range(MOBA_N_HEADS):
        sl = slice(h * dh, (h + 1) * dh)
        q_row = jnp.sum(jnp.where(is_t, q_ref[0, :, sl], 0.0), axis=0, keepdims=True)
        q_rep = jnp.broadcast_to(q_row, (dh, dh)).astype(BF16)
        raw.append((_dot_nt(kn_ref[0, :, sl].astype(BF16), q_rep),
                    [_dot_nt(kbuf[slot, h, n].astype(BF16), q_rep) for n in range(N_SEL_PAGES)]))
    for h in range(MOBA_N_HEADS):
        sl = slice(h * dh, (h + 1) * dh)
        s_own = jnp.where(own_ok, raw[h][0] * scale, NEG_BIG)
        s_parts = [sp * scale for sp in raw[h][1]]
        m = jnp.max(s_own, axis=0, keepdims=True)
        for sp in s_parts:
            m = jnp.maximum(m, jnp.max(sp, axis=0, keepdims=True))
        p_own = jnp.exp(s_own - m)
        l = jnp.sum(p_own, axis=0, keepdims=True)
        acc = jnp.sum(p_own * vn_ref[0, :, sl], axis=0, keepdims=True)
        for n, sp in enumerate(s_parts):
            pp = jnp.exp(sp - m)
            l = l + jnp.sum(pp, axis=0, keepdims=True)
            acc = acc + jnp.sum(pp * vbuf[slot, h, n], axis=0, keepdims=True)
        o_ref[0, :, sl] = jnp.where(is_t, acc / l, o_ref[0, :, sl])


def _moba_sample_call(page_table, sel, p3, k3, v3, cache_k, cache_v, layer, *, t_valid):
    bs, rows, _ = p3.shape
    W = MOBA_WIDTH

    def new_spec(col):
        return pl.BlockSpec((1, rows, W), lambda s, pt, sl: (s // t_valid, 0, col // W))

    buf_shape = (2, MOBA_N_HEADS, N_SEL_PAGES, PAGE_SIZE, MOBA_HEAD_DIM)
    grid_spec = pltpu.PrefetchScalarGridSpec(
        num_scalar_prefetch=2,
        grid=(bs * t_valid,),
        in_specs=[new_spec(COL_Q), new_spec(0), new_spec(0),
                  pl.BlockSpec(memory_space=pl.ANY), pl.BlockSpec(memory_space=pl.ANY)],
        out_specs=pl.BlockSpec((1, rows, W), lambda s, pt, sl: (s // t_valid, 0, 0)),
        scratch_shapes=[pltpu.VMEM(buf_shape, F32), pltpu.VMEM(buf_shape, F32),
                        pltpu.SemaphoreType.DMA((2,))],
    )
    return pl.pallas_call(
        functools.partial(_moba_sample_kernel, layer=layer, t_valid=t_valid),
        grid_spec=grid_spec,
        out_shape=_sds((bs, rows, W)),
        compiler_params=_cparams(("arbitrary",)),
        name="moba_sample",
    )(page_table, sel.reshape(-1), p3, k3, v3, cache_k, cache_v)


def _merge_kernel(ys_ref, ym_ref, yl_ref, g0_ref, g1_ref, g2_ref, x_ref, gt_ref, gp_ref,
                  ws_ref, wm_ref, wl_ref, wo_ref, o_ref):
    merged = (jax.nn.sigmoid(g0_ref[...]) * _dot(ys_ref[...].astype(BF16), ws_ref[...])
              + jax.nn.sigmoid(g1_ref[...]) * _dot(ym_ref[...].astype(BF16), wm_ref[...])
              + jax.nn.sigmoid(g2_ref[...]) * _dot(yl_ref[...].astype(BF16), wl_ref[...]))
    mix = _dot(merged.astype(BF16), wo_ref[...])
    o_ref[...] = x_ref[...] + gt_ref[0] * (_rms(mix) * gp_ref[...])


def _merge_call(ys, ym, yl, p2, x2, mod, gp, ws, wm, wl, wo, *, tm, tiles_per_mod):
    rows, d = x2.shape
    row_spec = pl.BlockSpec((tm, d), lambda i: (i, 0))
    w_spec = pl.BlockSpec((d, d), lambda i: (0, 0), pipeline_mode=pl.Buffered(1))

    def gate_spec(n):
        return pl.BlockSpec((tm, d), lambda i: (i, COL_GATE // d + n))

    return pl.pallas_call(
        _merge_kernel,
        grid=(rows // tm,),
        in_specs=[row_spec, row_spec, row_spec, gate_spec(0), gate_spec(1), gate_spec(2), row_spec,
                  _mod_spec(mod, tiles_per_mod, MOD_GT1, d, 1),
                  pl.BlockSpec((1, d), lambda i: (0, 0)),
                  w_spec, w_spec, w_spec, w_spec],
        out_specs=row_spec,
        out_shape=_sds((rows, d)),
        compiler_params=_cparams(("arbitrary",)),
        name="merge",
    )(ys, ym, yl, p2, p2, p2, x2, mod, gp, ws, wm, wl, wo)


def _mlp_kernel(x_ref, sc_ref, sh_ref, gt_ref, g1_ref, g2_ref, wu_ref, wd_ref, o_ref):
    x = x_ref[...]
    h2 = (_rms(x) * g1_ref[...]) * (1.0 + sc_ref[0]) + sh_ref[0]
    up = jnp.maximum(_dot(h2.astype(BF16), wu_ref[...]), 0.0)
    f = _dot((up * up).astype(BF16), wd_ref[...])
    o_ref[...] = x + gt_ref[0] * (_rms(f) * g2_ref[...])


def _mlp_call(x2, mod, g1, g2, wu, wd, *, tm, tiles_per_mod):
    rows, d = x2.shape
    hid = wu.shape[1]
    row_spec = pl.BlockSpec((tm, d), lambda i: (i, 0))
    vec_spec = pl.BlockSpec((1, d), lambda i: (0, 0))
    return pl.pallas_call(
        _mlp_kernel,
        grid=(rows // tm,),
        in_specs=[row_spec,
                  _mod_spec(mod, tiles_per_mod, MOD_SC2, d, 1),
                  _mod_spec(mod, tiles_per_mod, MOD_SH2, d, 1),
                  _mod_spec(mod, tiles_per_mod, MOD_GT2, d, 1),
                  vec_spec, vec_spec,
                  pl.BlockSpec((d, hid), lambda i: (0, 0), pipeline_mode=pl.Buffered(1)),
                  pl.BlockSpec((hid, d), lambda i: (0, 0), pipeline_mode=pl.Buffered(1))],
        out_specs=row_spec,
        out_shape=_sds((rows, d)),
        compiler_params=_cparams(("arbitrary",)),
        name="mlp",
    )(x2, mod, mod, mod, g1, g2, wu, wd)


def _block_diag_tiles(w, per_tile):
    nblk, d, _ = w.shape
    wt = w.reshape(nblk // per_tile, per_tile, d, d)
    eye = jnp.eye(per_tile, dtype=w.dtype)
    bd = wt[:, :, :, None, :] * eye[None, :, None, :, None]
    return bd.reshape(nblk // per_tile, per_tile * d, per_tile * d)


def _layer_weights(l, w_in, w_branch_ssd, w_branch_moba, w_branch_lru, w_out, w_up, w_down,
                   lru_wa, lru_wx):
    o = [0]
    for s in (SSD_D_INNER, SSD_CONV_DIM, SSD_N_HEADS, MOBA_WIDTH, MOBA_WIDTH, MOBA_WIDTH,
              LRU_WIDTH, LRU_WIDTH, N_BRANCH * D_MODEL):
        o.append(o[-1] + s)
    w = w_in[l]
    seg = lambda n: w[:, o[n]:o[n + 1]]
    w_all = jnp.concatenate([seg(1), seg(0), seg(3), seg(6), seg(7), seg(8), seg(4), seg(5)],
                            axis=1).astype(BF16)
    wdt = jnp.pad(seg(2), ((0, 0), (0, DT_PAD - SSD_N_HEADS))).astype(BF16)
    per_tile = 256 // LRU_BLOCK
    return dict(
        w_all=w_all, wdt=wdt,
        ws=w_branch_ssd[l].astype(BF16), wm=w_branch_moba[l].astype(BF16),
        wl=w_branch_lru[l].astype(BF16), wo=w_out[l].astype(BF16),
        wu=w_up[l].astype(BF16), wd=w_down[l].astype(BF16),
        wa_bd=_block_diag_tiles(lru_wa[l], per_tile).astype(BF16),
        wx_bd=_block_diag_tiles(lru_wx[l], per_tile).astype(BF16),
    )


def _row(v):
    return v.reshape(1, -1)


def _pad_lanes(v, width):
    return jnp.pad(v, (0, width - v.shape[0])).reshape(1, width)


def _halo(buf):
    return jnp.pad(buf, ((0, 0), (CONV_HALO - buf.shape[1], 0), (0, 0)))


def _group_layer(x3, mods, wts, prm, ssd_buf, ssd_h, ssd_h_layer, lru_buf, lru_h, moba_fn, *, t_valid,
                 tm_proj, tm_dense, ssd_rows, ssd_seqs, lru_tm, rows_per_mod):
    bsz, t_pad, d = x3.shape
    rows = bsz * t_pad
    x2 = x3.reshape(rows, d)
    tiles_per_mod = rows_per_mod // tm_dense
    p2, dt2, k2, v2 = _inproj_call(x2, mods, _row(prm['g_pre_mix']), wts['w_all'], wts['wdt'],
                                   tm=tm_proj, tiles_per_mod=rows_per_mod // tm_proj)
    p3 = p2.reshape(bsz, t_pad, P_WIDTH)
    dt3 = dt2.reshape(bsz, t_pad, DT_PAD)
    k3 = k2.reshape(bsz, t_pad, MOBA_WIDTH)
    v3 = v2.reshape(bsz, t_pad, MOBA_WIDTH)

    y_ssd, ssd_h_new = _ssd_call(
        p3, dt3, _halo(ssd_buf),
        ssd_h.reshape(ssd_h.shape[0], bsz, SSD_N_HEADS * SSD_HEAD_DIM, SSD_D_STATE), ssd_h_layer,
        prm['ssd_conv_w'], _row(prm['ssd_conv_b']), _pad_lanes(prm['ssd_dt_bias'], DT_PAD),
        _pad_lanes(prm['ssd_a_log'], DT_PAD), _row(jnp.repeat(prm['ssd_d'], SSD_HEAD_DIM)),
        _row(prm['ssd_norm_g']), rows_in=ssd_rows, t_valid=t_valid, seqs=ssd_seqs)

    y_lru, lru_h_new = _lru_call(
        p3, _halo(lru_buf), lru_h.reshape(bsz, 1, LRU_WIDTH), prm['lru_conv_w'],
        _row(prm['lru_conv_b']), wts['wa_bd'], wts['wx_bd'], _row(prm['lru_ba']),
        _row(prm['lru_bx']), _row(prm['lru_lambda']), tm=lru_tm, t_valid=t_valid)

    y_moba = moba_fn(p3, k3, v3)

    x1 = _merge_call(y_ssd.reshape(rows, d), y_moba.reshape(rows, d), y_lru.reshape(rows, d), p2, x2,
                     mods, _row(prm['g_post_mix']), wts['ws'], wts['wm'], wts['wl'], wts['wo'],
                     tm=tm_dense, tiles_per_mod=tiles_per_mod)
    x_out = _mlp_call(x1, mods, _row(prm['g_pre_mlp']), _row(prm['g_post_mlp']),
                      wts['wu'], wts['wd'], tm=tm_dense, tiles_per_mod=tiles_per_mod)

    lo = t_valid - (SSD_CONV - 1)
    k_new = k3[:, :t_valid].reshape(bsz, t_valid, MOBA_N_HEADS, MOBA_HEAD_DIM)
    v_new = v3[:, :t_valid].reshape(bsz, t_valid, MOBA_N_HEADS, MOBA_HEAD_DIM)
    ssd_buf_new = p3[:, lo:t_valid, COL_XBC:COL_XBC + SSD_CONV_DIM]
    lru_buf_new = p3[:, lo:t_valid, COL_XR:COL_XR + LRU_WIDTH]
    return (x_out.reshape(bsz, t_pad, d), k_new, v_new, ssd_buf_new,
            ssd_h_new.reshape(bsz, SSD_N_HEADS, SSD_HEAD_DIM, SSD_D_STATE), lru_buf_new,
            lru_h_new.reshape(bsz, LRU_WIDTH))


def kernel(x_prompt, x_sample, c_prompt, c_sample, cache_k, cache_v, page_table, state_ssm, state_ssm_conv, state_lru, state_lru_conv, w_mod, b_mod, g_pre_mix, g_post_mix, g_pre_mlp, g_post_mlp, w_in, ssd_conv_w, ssd_conv_b, ssd_dt_bias, ssd_a_log, ssd_d, ssd_norm_g, lru_conv_w, lru_conv_b, lru_wa, lru_ba, lru_wx, lru_bx, lru_lambda, w_branch_ssd, w_branch_moba, w_branch_lru, w_out, w_up, w_down):
    bp, tp, d = x_prompt.shape
    bs, ts, _ = x_sample.shape
    depth = w_in.shape[0]
    n_pool = cache_k.shape[1]
    assert ts <= SAMPLE_T_PAD and ts >= SSD_CONV - 1
    assert (page_table.shape[1] * PAGE_SIZE) % MOBA_BLOCK == 0
    assert tp % SSD_CHUNK == 0 and tp % MOBA_BLOCK == 0

    tm_p = min(1024, tp)
    tm_dense_p = min(512, tp)
    lru_tm_p = min(256, tp)

    n_c = bp + bs
    n_c_pad = -(-n_c // SUBLANES) * SUBLANES
    c_all = jnp.pad(jnp.concatenate([c_prompt, c_sample], axis=0), ((0, n_c_pad - n_c), (0, 0)))
    mod_all = _mod_call(c_all, w_mod, b_mod)

    xs_pad = jnp.pad(x_sample, ((0, 0), (0, SAMPLE_T_PAD - ts), (0, 0)))

    zero_ssd_buf = jnp.zeros((bp, SSD_CONV - 1, SSD_CONV_DIM), F32)
    zero_ssd_h = jnp.zeros((1, bp, SSD_N_HEADS, SSD_HEAD_DIM, SSD_D_STATE), F32)
    zero_lru_buf = jnp.zeros((bp, LRU_CONV - 1, LRU_WIDTH), F32)
    zero_lru_h = jnp.zeros((bp, LRU_WIDTH), F32)

    outs_p = [[] for _ in range(6)]
    outs_s = [[] for _ in range(6)]
    xp, xs = x_prompt, xs_pad
    for l in range(depth):
        prm = {
            'g_pre_mix': g_pre_mix[l], 'g_post_mix': g_post_mix[l], 'g_pre_mlp': g_pre_mlp[l],
            'g_post_mlp': g_post_mlp[l], 'ssd_conv_w': ssd_conv_w[l], 'ssd_conv_b': ssd_conv_b[l],
            'ssd_dt_bias': ssd_dt_bias[l], 'ssd_a_log': ssd_a_log[l], 'ssd_d': ssd_d[l],
            'ssd_norm_g': ssd_norm_g[l], 'lru_conv_w': lru_conv_w[l], 'lru_conv_b': lru_conv_b[l],
            'lru_ba': lru_ba[l].reshape(-1), 'lru_bx': lru_bx[l].reshape(-1),
            'lru_lambda': lru_lambda[l],
        }
        wts = _layer_weights(l, w_in, w_branch_ssd, w_branch_moba, w_branch_lru, w_out, w_up,
                             w_down, lru_wa, lru_wx)
        mod_l = mod_all[l]
        mods_p = mod_l[:bp].reshape(bp, 1, 6 * d)
        mods_s = jnp.repeat(mod_l[bp:bp + bs], SAMPLE_T_PAD, axis=0).reshape(1, bs * SAMPLE_T_PAD, 6 * d)

        ksum_box = []

        def moba_prompt(p3, k3, v3, l=l, ksum_box=ksum_box):
            y, ksum = _moba_prompt_call(p3, k3, v3, page_table, cache_k, l)
            ksum_box.append(ksum)
            return y

        res_p = _group_layer(xp, mods_p, wts, prm, zero_ssd_buf, zero_ssd_h, 0, zero_lru_buf, zero_lru_h,
                             moba_prompt, t_valid=tp, tm_proj=tm_p, tm_dense=tm_dense_p,
                             ssd_rows=SSD_CHUNK, ssd_seqs=1, lru_tm=lru_tm_p, rows_per_mod=tp)
        xp = res_p[0]
        for n in range(6):
            outs_p[n].append(res_p[n + 1])

        def moba_sample(p3, k3, v3, l=l, ksum_box=ksum_box):
            sel = _select_call(p3, ksum_box[0])
            sel = sel[:, :ts, :MOBA_N_HEADS * 4].reshape(bs, ts, MOBA_N_HEADS, 4)[..., :MOBA_TOPK]
            return _moba_sample_call(page_table, sel, p3, k3, v3, cache_k, cache_v, l, t_valid=ts)

        res_s = _group_layer(xs, mods_s, wts, prm, state_ssm_conv[l], state_ssm, l, state_lru_conv[l],
                             state_lru[l], moba_sample, t_valid=ts, tm_proj=bs * SAMPLE_T_PAD,
                             tm_dense=bs * SAMPLE_T_PAD, ssd_rows=SAMPLE_T_PAD, ssd_seqs=2,
                             lru_tm=SAMPLE_T_PAD,
                             rows_per_mod=bs * SAMPLE_T_PAD)
        xs = res_s[0]
        for n in range(6):
            outs_s[n].append(res_s[n + 1])

    st = lambda lst: jnp.stack(lst)
    kp, vp, ssmc_p, ssm_p, lruc_p, lru_p = outs_p
    ks, vs, ssmc_s, ssm_s, lruc_s, lru_s = outs_s
    return (xp, xs[:, :ts],
            st(kp), st(vp), st(ks), st(vs),
            st(ssm_p), st(ssmc_p), st(lru_p), st(lruc_p),
            st(ssm_s), st(ssmc_s), st(lru_s), st(lruc_s))
```

```python
import functools
import math

import jax
import jax.numpy as jnp
from jax import lax
from jax.experimental import pallas as pl
from jax.experimental.pallas import tpu as pltpu

F32 = jnp.float32
BF16 = jnp.bfloat16
HIGHEST = lax.Precision.HIGHEST

D_MODEL = 1024
PAGE_SIZE = 128
SSD_D_INNER = D_MODEL
SSD_HEAD_DIM = 64
SSD_N_HEADS = SSD_D_INNER // SSD_HEAD_DIM
SSD_N_GROUPS = 4
SSD_D_STATE = 128
SSD_CONV = 4
SSD_CHUNK = 128
SSD_CONV_DIM = SSD_D_INNER + 2 * SSD_N_GROUPS * SSD_D_STATE
MOBA_N_HEADS = 8
MOBA_HEAD_DIM = 128
MOBA_WIDTH = MOBA_N_HEADS * MOBA_HEAD_DIM
MOBA_BLOCK = 256
MOBA_TOPK = 3
LRU_WIDTH = D_MODEL
LRU_N_BLOCKS = 16
LRU_BLOCK = LRU_WIDTH // LRU_N_BLOCKS
LRU_CONV = 4
LRU_C = 8.0
N_BRANCH = 3
MLP_HIDDEN = 4 * D_MODEL
RMS_EPS = 1e-6

SUBLANES = 8
LANES = 128
VMEM_LIMIT_BYTES = 56 * 1024 * 1024

SAMPLE_T_PAD = SUBLANES
CONV_HALO = SUBLANES
NEG_BIG = -1e30
LOG2E = math.log2(math.e)

COL_XBC = 0
COL_Z = COL_XBC + SSD_CONV_DIM
COL_Q = COL_Z + SSD_D_INNER
COL_XR = COL_Q + MOBA_WIDTH
COL_GR = COL_XR + LRU_WIDTH
COL_GATE = COL_GR + LRU_WIDTH
P_WIDTH = COL_GATE + N_BRANCH * D_MODEL
PROJ_TN = 1024
P_TILES = P_WIDTH // PROJ_TN
DT_PAD = LANES


def _cparams(sem):
    return pltpu.CompilerParams(dimension_semantics=sem, vmem_limit_bytes=VMEM_LIMIT_BYTES)


def _sds(shape, dtype=F32):
    return jax.ShapeDtypeStruct(shape, dtype)


def _silu(x):
    return x * jax.nn.sigmoid(x)


def _softplus(x):
    return jnp.maximum(x, 0.0) + jnp.log1p(jnp.exp(-jnp.abs(x)))


def _rms(x):
    return x * lax.rsqrt(jnp.mean(x * x, axis=-1, keepdims=True) + RMS_EPS)


def _dot(a, b):
    return jnp.dot(a, b, preferred_element_type=F32)


def _dot_nt(a, b, precision=None):
    return lax.dot_general(a, b, (((1,), (1,)), ((), ())), precision=precision,
                           preferred_element_type=F32)


def _mod_kernel(c_ref, w_ref, b_ref, o_ref):
    s = _silu(c_ref[...]).astype(BF16)
    o_ref[0] = _dot(s, w_ref[0].astype(BF16)) + b_ref[0]


def _mod_call(c_all, w_mod, b_mod):
    depth, d, n = w_mod.shape
    rows = c_all.shape[0]
    tn = 1536
    return pl.pallas_call(
        _mod_kernel,
        grid=(depth, n // tn),
        in_specs=[pl.BlockSpec((rows, d), lambda l, j: (0, 0)),
                  pl.BlockSpec((1, d, tn), lambda l, j: (l, 0, j)),
                  pl.BlockSpec((1, 1, tn), lambda l, j: (l, 0, j))],
        out_specs=pl.BlockSpec((1, rows, tn), lambda l, j: (l, 0, j)),
        out_shape=_sds((depth, rows, n)),
        compiler_params=_cparams(("arbitrary", "arbitrary")),
        name="mod",
    )(c_all, w_mod, b_mod.reshape(depth, 1, n))


def _inproj_kernel(x_ref, sc_ref, sh_ref, g_ref, w_ref, wdt_ref, p_ref, dt_ref, k_ref, v_ref, h_scr):
    j = pl.program_id(1)

    @pl.when(j == 0)
    def _():
        y = _rms(x_ref[...]) * g_ref[...]
        hb = (y * (1.0 + sc_ref[0]) + sh_ref[0]).astype(BF16)
        h_scr[...] = hb
        dt_ref[...] = _dot(hb, wdt_ref[...])

    @pl.when(j < P_TILES)
    def _():
        p_ref[...] = _dot(h_scr[...], w_ref[...])

    @pl.when(j == P_TILES)
    def _():
        k_ref[...] = _dot(h_scr[...], w_ref[...])

    @pl.when(j == P_TILES + 1)
    def _():
        v_ref[...] = _dot(h_scr[...], w_ref[...])


MOD_SH1, MOD_SC1, MOD_GT1, MOD_SH2, MOD_SC2, MOD_GT2 = range(6)


def _mod_spec(mod, tiles_per_mod, which, d, n_grid):
    shape = (1, mod.shape[1], d)
    if n_grid == 2:
        return pl.BlockSpec(shape, lambda i, j: (i // tiles_per_mod, 0, which))
    return pl.BlockSpec(shape, lambda i: (i // tiles_per_mod, 0, which))


def _inproj_call(x2, mod, g, w_all, wdt, *, tm, tiles_per_mod):
    rows, d = x2.shape
    assert PROJ_TN == MOBA_WIDTH and w_all.shape[1] == P_WIDTH + 2 * MOBA_WIDTH
    kv_spec = pl.BlockSpec((tm, MOBA_WIDTH), lambda i, j: (i, 0))
    return pl.pallas_call(
        _inproj_kernel,
        grid=(rows // tm, P_TILES + 2),
        in_specs=[pl.BlockSpec((tm, d), lambda i, j: (i, 0)),
                  _mod_spec(mod, tiles_per_mod, MOD_SC1, d, 2),
                  _mod_spec(mod, tiles_per_mod, MOD_SH1, d, 2),
                  pl.BlockSpec((1, d), lambda i, j: (0, 0)),
                  pl.BlockSpec((d, PROJ_TN), lambda i, j: (0, j)),
                  pl.BlockSpec((d, DT_PAD), lambda i, j: (0, 0))],
        out_specs=[pl.BlockSpec((tm, PROJ_TN), lambda i, j: (i, jnp.minimum(j, P_TILES - 1))),
                   pl.BlockSpec((tm, DT_PAD), lambda i, j: (i, 0)),
                   kv_spec, kv_spec],
        out_shape=[_sds((rows, P_WIDTH)), _sds((rows, DT_PAD)),
                   _sds((rows, MOBA_WIDTH)), _sds((rows, MOBA_WIDTH))],
        scratch_shapes=[pltpu.VMEM((tm, d), BF16)],
        compiler_params=_cparams(("arbitrary", "arbitrary")),
        name="inproj",
    )(x2, mod, mod, g, w_all, wdt)


def _conv_tile(prev_scr, x, buf_ref, w_ref, b_ref, tile_idx):
    rows, width = x.shape

    @pl.when(tile_idx == 0)
    def _():
        prev_scr[...] = buf_ref[0]

    prev = prev_scr[...]
    prev_scr[...] = x[rows - CONV_HALO:, :]
    n_groups = rows // CONV_HALO
    x3 = jnp.concatenate([prev, x], axis=0).reshape(n_groups + 1, CONV_HALO, width)
    row = lax.broadcasted_iota(jnp.int32, (1, CONV_HALO, width), 1)
    acc = b_ref[...] + w_ref[SSD_CONV - 1:SSD_CONV, :] * x
    for k in range(1, SSD_CONV):
        r3 = pltpu.roll(x3, k, 1)
        shifted = jnp.where(row >= k, r3[1:], r3[:-1]).reshape(rows, width)
        acc = acc + w_ref[SSD_CONV - 1 - k:SSD_CONV - k, :] * shifted
    return acc


def _ssd_kernel(xbc_ref, z_ref, dt_ref, buf_ref, h0_ref, cw_ref, cb_ref, dtb_ref, alog_ref,
                dsk_ref, ng_ref, y_ref, h_ref, xp_scr, dt_scr, *, rows_in, t_valid, n_seq):
    L = SSD_CHUNK
    N = SSD_D_STATE
    P2 = 2 * SSD_HEAD_DIM
    c = pl.program_id(1)

    @pl.when(c == 0)
    def _():
        h_ref[...] = h0_ref[...]

    lane = lax.broadcasted_iota(jnp.int32, (L, LANES), 1)
    row = lax.broadcasted_iota(jnp.int32, (L, LANES), 0)
    ri = lax.broadcasted_iota(jnp.int32, (L, L), 0)
    ci = lax.broadcasted_iota(jnp.int32, (L, L), 1)
    causal = ri >= ci
    left = lax.broadcasted_iota(jnp.int32, (L, P2), 1) < SSD_HEAD_DIM
    left_rows = lax.broadcasted_iota(jnp.int32, (P2, N), 0) < SSD_HEAD_DIM
    a = jnp.where(lane < SSD_N_HEADS, -jnp.exp(alog_ref[...]), 0.0)

    ctx = []
    for s in range(n_seq):
        xc = _silu(_conv_tile(xp_scr.at[s], xbc_ref[s], buf_ref.at[pl.ds(s, 1)], cw_ref, cb_ref, c))
        if rows_in < L:
            xc = jnp.concatenate([xc, jnp.zeros((L - rows_in, SSD_CONV_DIM), F32)], axis=0)
            dt_scr[s] = jnp.zeros((L, LANES), F32)
        dt_scr[s, 0:rows_in, :] = dt_ref[s]
        dtv = _softplus(dt_scr[s] + dtb_ref[...])
        dtv = jnp.where((row + c * L < t_valid) & (lane < SSD_N_HEADS), dtv, 0.0)
        cum = jnp.dot(causal.astype(F32), dtv * a, precision=HIGHEST, preferred_element_type=F32)
        cum_t = cum.T
        cum_last = cum[L - 1:L, :]
        ctx.append(dict(
            xs=xc[:, :SSD_D_INNER],
            bm=xc[:, SSD_D_INNER:SSD_D_INNER + SSD_N_GROUPS * N],
            cm=xc[:, SSD_D_INNER + SSD_N_GROUPS * N:],
            cum=cum, cum_t=cum_t, dtv_t=dtv.T, ecum=jnp.exp(cum),
            wgt=jnp.exp(cum_last - cum) * dtv,
            cdec_t=jnp.exp(cum_t[:, L - 1:L]),
            y_tiles=[]))

    for j in range(SSD_N_HEADS // 2):
        g = j // (SSD_N_HEADS // SSD_N_GROUPS // 2)
        h_a, h_b = 2 * j, 2 * j + 1
        for s in range(n_seq):
            k = ctx[s]
            bm_g = k['bm'][:, g * N:(g + 1) * N]
            cm_g = k['cm'][:, g * N:(g + 1) * N].astype(BF16)
            cb = _dot_nt(cm_g, bm_g.astype(BF16))
            xs_p = k['xs'][:, j * P2:(j + 1) * P2]
            m_parts = []
            for hh in (h_a, h_b):
                seg = k['cum'][:, hh:hh + 1] - k['cum_t'][hh:hh + 1, :]
                dec = jnp.exp(jnp.where(causal, seg, -jnp.inf))
                m_parts.append((cb * dec * k['dtv_t'][hh:hh + 1, :]).astype(BF16))
            m_pair = jnp.concatenate(m_parts, axis=1)
            x_bd = jnp.concatenate([jnp.where(left, xs_p, 0.0), jnp.where(left, 0.0, xs_p)],
                                   axis=0).astype(BF16)
            y_diag = _dot(m_pair, x_bd)
            h_prev = h_ref[s, j * P2:(j + 1) * P2, :]
            y_off = _dot_nt(cm_g, h_prev.astype(BF16))
            e_sel = jnp.where(left, k['ecum'][:, h_a:h_a + 1], k['ecum'][:, h_b:h_b + 1])
            k['y_tiles'].append(y_diag + y_off * e_sel + dsk_ref[:, j * P2:(j + 1) * P2] * xs_p)
            w_sel = jnp.where(left, k['wgt'][:, h_a:h_a + 1], k['wgt'][:, h_b:h_b + 1])
            xw_t = (xs_p * w_sel).T.astype(BF16)
            st = _dot(xw_t, bm_g.astype(BF16))
            d_sel = jnp.where(left_rows, k['cdec_t'][h_a:h_a + 1, :], k['cdec_t'][h_b:h_b + 1, :])
            h_ref[s, j * P2:(j + 1) * P2, :] = h_prev * d_sel + st

    gw = SSD_D_INNER // SSD_N_GROUPS
    tiles_per_group = gw // P2
    for s in range(n_seq):
        zs = _silu(z_ref[s])
        for g in range(SSD_N_GROUPS):
            yg = jnp.concatenate(ctx[s]['y_tiles'][g * tiles_per_group:(g + 1) * tiles_per_group], axis=1)
            yg = yg[0:rows_in, :] * zs[:, g * gw:(g + 1) * gw]
            y_ref[s, :, g * gw:(g + 1) * gw] = (_rms(yg) * ng_ref[:, g * gw:(g + 1) * gw]).astype(BF16)


def _ssd_call(p3, dt3, buf8, h0, h0_layer, cw, cb, dtb, alog, dsk, ng, *, rows_in, t_valid, seqs):
    bsz, t_pad, _ = p3.shape
    nc = t_pad // rows_in
    L = SSD_CHUNK
    hp = SSD_N_HEADS * SSD_HEAD_DIM
    ns = math.gcd(seqs, bsz)
    const2 = lambda b, c: (0, 0)
    kern = functools.partial(_ssd_kernel, rows_in=rows_in, t_valid=t_valid, n_seq=ns)
    return pl.pallas_call(
        kern,
        grid=(bsz // ns, nc),
        in_specs=[pl.BlockSpec((ns, rows_in, SSD_CONV_DIM), lambda b, c: (b, c, COL_XBC // SSD_CONV_DIM)),
                  pl.BlockSpec((ns, rows_in, SSD_D_INNER), lambda b, c: (b, c, COL_Z // SSD_D_INNER)),
                  pl.BlockSpec((ns, rows_in, DT_PAD), lambda b, c: (b, c, 0)),
                  pl.BlockSpec((ns, CONV_HALO, SSD_CONV_DIM), lambda b, c: (b, 0, 0)),
                  pl.BlockSpec((None, ns, hp, SSD_D_STATE), lambda b, c: (h0_layer, b, 0, 0)),
                  pl.BlockSpec((SSD_CONV, SSD_CONV_DIM), const2),
                  pl.BlockSpec((1, SSD_CONV_DIM), const2),
                  pl.BlockSpec((1, DT_PAD), const2),
                  pl.BlockSpec((1, DT_PAD), const2),
                  pl.BlockSpec((1, SSD_D_INNER), const2),
                  pl.BlockSpec((1, SSD_D_INNER), const2)],
        out_specs=[pl.BlockSpec((ns, rows_in, SSD_D_INNER), lambda b, c: (b, c, 0)),
                   pl.BlockSpec((ns, hp, SSD_D_STATE), lambda b, c: (b, 0, 0))],
        out_shape=[_sds((bsz, t_pad, SSD_D_INNER), BF16), _sds((bsz, hp, SSD_D_STATE))],
        scratch_shapes=[pltpu.VMEM((ns, CONV_HALO, SSD_CONV_DIM), F32),
                        pltpu.VMEM((ns, L, DT_PAD), F32)],
        compiler_params=_cparams(("arbitrary", "arbitrary")),
        name="ssd",
    )(p3, p3, dt3, buf8, h0, cw, cb, dtb, alog, dsk, ng)


def _lru_kernel(xr_ref, gr_ref, buf_ref, h0_ref, cw_ref, cb_ref, wa_ref, wx_ref, ba_ref, bx_ref,
                lam_ref, y_ref, h_ref, xp_scr, *, tm, t_valid, t_pad):
    for s in range(xr_ref.shape[0]):
        one = pl.ds(s, 1)
        _lru_tile(xr_ref.at[one], gr_ref.at[one], buf_ref.at[one], h0_ref.at[one], cw_ref, cb_ref,
                  wa_ref, wx_ref, ba_ref, bx_ref, lam_ref, y_ref.at[one], h_ref.at[one],
                  xp_scr.at[s], tm=tm, t_valid=t_valid, t_pad=t_pad)


def _lru_tile(xr_ref, gr_ref, buf_ref, h0_ref, cw_ref, cb_ref, wa_ref, wx_ref, ba_ref, bx_ref,
              lam_ref, y_ref, h_ref, xp_scr, *, tm, t_valid, t_pad):
    i = pl.program_id(1)

    @pl.when(i == 0)
    def _():
        h_ref[0] = h0_ref[0]

    xc = _conv_tile(xp_scr, xr_ref[0], buf_ref, cw_ref, cb_ref, i)
    bw = wa_ref.shape[1]
    r_parts, i_parts = [], []
    for j in range(LRU_WIDTH // bw):
        xj = xc[:, j * bw:(j + 1) * bw].astype(BF16)
        r_parts.append(_dot(xj, wa_ref[j]))
        i_parts.append(_dot(xj, wx_ref[j]))
    r = jax.nn.sigmoid(jnp.concatenate(r_parts, axis=1) + ba_ref[...])
    ig = jax.nn.sigmoid(jnp.concatenate(i_parts, axis=1) + bx_ref[...])
    log_a = (-LRU_C) * r * _softplus(-lam_ref[...])
    a_full = jnp.exp(log_a)
    b_full = jnp.sqrt(-jnp.tanh(log_a) * (a_full * a_full + 1.0)) * (ig * xc)
    gate = jax.nn.gelu(gr_ref[0], approximate=True)

    n_groups = tm // SUBLANES
    row = lax.broadcasted_iota(jnp.int32, (tm, LANES), 0)
    row_in_group = lax.broadcasted_iota(jnp.int32, (1, SUBLANES, LANES), 1)
    for s in range(LRU_WIDTH // LANES):
        sl = slice(s * LANES, (s + 1) * LANES)
        a, b = a_full[:, sl], b_full[:, sl]
        if t_valid < t_pad:
            valid = row + i * tm < t_valid
            a = jnp.where(valid, a, 1.0)
            b = jnp.where(valid, b, 0.0)
        a = a.reshape(n_groups, SUBLANES, LANES)
        b = b.reshape(n_groups, SUBLANES, LANES)
        k = 1
        while k < SUBLANES:
            keep = row_in_group >= k
            b = b + a * jnp.where(keep, pltpu.roll(b, k, 1), 0.0)
            a = a * jnp.where(keep, pltpu.roll(a, k, 1), 1.0)
            k *= 2
        carry = h_ref[0, :, sl]
        groups = []
        for g in range(n_groups):
            hg = b[g] + a[g] * carry
            groups.append(hg)
            carry = hg[SUBLANES - 1:SUBLANES, :]
        h = groups[0] if n_groups == 1 else jnp.concatenate(groups, axis=0)
        y_ref[0, :, sl] = (h * gate[:, sl]).astype(BF16)
        h_ref[0, :, sl] = carry


def _lru_call(p3, buf8, h0, cw, cb, wa_bd, wx_bd, ba, bx, lam, *, tm, t_valid, seqs):
    bsz, t_pad, _ = p3.shape
    W = LRU_WIDTH
    nb, bw, _ = wa_bd.shape
    ns = math.gcd(seqs, bsz)
    const2 = lambda b, i: (0, 0)
    const3 = lambda b, i: (0, 0, 0)
    kern = functools.partial(_lru_kernel, tm=tm, t_valid=t_valid, t_pad=t_pad)
    return pl.pallas_call(
        kern,
        grid=(bsz // ns, t_pad // tm),
        in_specs=[pl.BlockSpec((ns, tm, W), lambda b, i: (b, i, COL_XR // W)),
                  pl.BlockSpec((ns, tm, W), lambda b, i: (b, i, COL_GR // W)),
                  pl.BlockSpec((ns, CONV_HALO, W), lambda b, i: (b, 0, 0)),
                  pl.BlockSpec((ns, 1, W), lambda b, i: (b, 0, 0)),
                  pl.BlockSpec((LRU_CONV, W), const2),
                  pl.BlockSpec((1, W), const2),
                  pl.BlockSpec((nb, bw, bw), const3),
                  pl.BlockSpec((nb, bw, bw), const3),
                  pl.BlockSpec((1, W), const2),
                  pl.BlockSpec((1, W), const2),
                  pl.BlockSpec((1, W), const2)],
        out_specs=[pl.BlockSpec((ns, tm, W), lambda b, i: (b, i, 0)),
                   pl.BlockSpec((ns, 1, W), lambda b, i: (b, 0, 0))],
        out_shape=[_sds((bsz, t_pad, W), BF16), _sds((bsz, 1, W))],
        scratch_shapes=[pltpu.VMEM((ns, CONV_HALO, W), F32)],
        compiler_params=_cparams(("arbitrary", "arbitrary")),
        name="lru",
    )(p3, p3, buf8, h0, cw, cb, wa_bd, wx_bd, ba, bx, lam)


def _topk_mask(gate, n_valid, col):
    nb = gate.shape[1]
    past = col < n_valid
    gm = jnp.where(past, gate, -jnp.inf)
    rank = jnp.zeros(gate.shape, F32)
    for j in range(nb):
        gj = gm[:, j:j + 1]
        ahead = (gj > gm) | ((gj == gm) & (col > j))
        rank = rank + jnp.where(ahead, 1.0, 0.0)
    return (rank < MOBA_TOPK) & past, rank


PAGES_PER_BLOCK = MOBA_BLOCK // PAGE_SIZE


def _ksum_stream(pt_ref, ck_hbm, ks_ref, kbuf, sem, step, n_steps, *, layer, pps, n_pages):
    def copies(st, slot, lookup):
        out = []
        for u in range(pps):
            if lookup:
                p = st * pps + u
                page = pt_ref[lax.div(p, n_pages), lax.rem(p, n_pages)]
            else:
                page = 0
            out.append(pltpu.make_async_copy(ck_hbm.at[layer, page], kbuf.at[slot, u], sem.at[slot]))
        return out

    slot = lax.rem(step, 2)

    @pl.when(step == 0)
    def _():
        for c in copies(step, slot, True):
            c.start()

    @pl.when(step + 1 < n_steps)
    def _():
        for c in copies(step + 1, 1 - slot, True):
            c.start()

    for c in copies(step, slot, False):
        c.wait()

    blk0 = lax.div(lax.rem(step * pps, n_pages), PAGES_PER_BLOCK)

    @pl.when(blk0 == 0)
    def _():
        ks_ref[...] = jnp.zeros(ks_ref.shape, F32)

    sums = []
    for r in range(pps // PAGES_PER_BLOCK):
        acc = jnp.sum(kbuf[slot, r * PAGES_PER_BLOCK], axis=0)
        for u in range(1, PAGES_PER_BLOCK):
            acc = acc + jnp.sum(kbuf[slot, r * PAGES_PER_BLOCK + u], axis=0)
        sums.append(acc)
    row = lax.broadcasted_iota(jnp.int32, ks_ref.shape[2:], 0)
    for h in range(MOBA_N_HEADS):
        cur = ks_ref[0, h]
        for r, acc in enumerate(sums):
            cur = jnp.where(row == blk0 + r, acc[h:h + 1, :], cur)
        ks_ref[0, h] = cur


def _moba_prompt_kernel(pt_ref, q_ref, k_ref, v_ref, ck_hbm, o_ref, ks_ref, kmean_scr, kg_scr, vtg_scr,
                        sel_scr, kbuf, sem, *, nb, grp, nh, layer, pps, n_pages):
    i = pl.program_id(2)
    BS = MOBA_BLOCK
    GB = grp * BS
    dh = MOBA_HEAD_DIM
    scale = MOBA_HEAD_DIM ** -0.5

    step = (pl.program_id(0) * pl.num_programs(1) + pl.program_id(1)) * pl.num_programs(2) + i
    n_steps = pl.num_programs(0) * pl.num_programs(1) * pl.num_programs(2)
    _ksum_stream(pt_ref, ck_hbm, ks_ref, kbuf, sem, step, n_steps, layer=layer, pps=pps,
                 n_pages=n_pages)

    @pl.when(i == 0)
    def _():
        for hh in range(nh):
            hs = slice(hh * dh, (hh + 1) * dh)
            for j in range(nb):
                kmean_scr[hh, j:j + 1, :] = jnp.mean(k_ref[0, j * BS:(j + 1) * BS, hs], axis=0,
                                                     keepdims=True)
            for jj in range(nb // grp):
                kg_scr[hh, jj] = k_ref[0, jj * GB:(jj + 1) * GB, hs].astype(BF16)
                for u in range(grp):
                    r0 = jj * GB + u * BS
                    vtg_scr[hh, jj, :, u * BS:(u + 1) * BS] = v_ref[0, r0:r0 + BS, hs].T.astype(BF16)

    qb_t = []
    for hh in range(nh):
        q_t = q_ref[0, :, hh * dh:(hh + 1) * dh].T
        qb_t.append(q_t.astype(BF16))
        gate = jnp.dot(kmean_scr[hh], q_t, precision=HIGHEST, preferred_element_type=F32)
        blk = lax.broadcasted_iota(jnp.int32, gate.shape, 0)
        past = blk < i
        gm = jnp.where(past, gate, -jnp.inf)
        rank = jnp.zeros(gate.shape, F32)
        for j in range(nb):
            gj = gm[j:j + 1, :]
            ahead = (gj > gm) | ((gj == gm) & (blk > j))
            rank = rank + jnp.where(ahead, 1.0, 0.0)
        sel = jnp.where((rank < MOBA_TOPK) & past, 0.0, -jnp.inf)
        for j in range(nb):
            sel_scr[hh, j] = jnp.broadcast_to(sel[j:j + 1, :], (SUBLANES, BS))

    def block_bias(hh, j):
        rows = sel_scr[hh, j]
        return jnp.broadcast_to(rows[None], (BS // SUBLANES, SUBLANES, BS)).reshape(BS, BS)

    sub = min(MOBA_SUB, grp)
    n_parts = grp // sub
    SB = sub * BS

    def scores(hh, jj, part):
        return _dot(kg_scr[hh, jj, part * SB:(part + 1) * SB, :], qb_t[hh])

    def update(carry, hh, jj, part, s_raw, bias):
        m, l, acc = carry
        s = s_raw * (scale * LOG2E) + bias
        m_new = jnp.maximum(m, jnp.max(s, axis=0, keepdims=True))
        alpha = jnp.exp2(m - m_new)
        p = jnp.exp2(s - m_new)
        l_new = alpha * l + jnp.sum(p, axis=0, keepdims=True)
        v_t = vtg_scr[hh, jj, :, part * SB:(part + 1) * SB]
        acc_new = alpha * acc + _dot(v_t, p.astype(BF16))
        return m_new, l_new, acc_new

    def run_parts(jj, carries, bias_fn, parts):
        s_raw = {part: [scores(hh, jj, part) for hh in range(nh)] for part in parts}
        carries = list(carries)
        for part in parts:
            for hh in range(nh):
                bias = jnp.concatenate(
                    [bias_fn(hh, jj * grp + part * sub + u) for u in range(sub)], axis=0)
                carries[hh] = update(carries[hh], hh, jj, part, s_raw[part][hh], bias)
        return tuple(carries)

    def body(jj, carries):
        return run_parts(jj, carries, block_bias, range(n_parts))

    init = tuple((jnp.full((1, BS), NEG_BIG, F32), jnp.zeros((1, BS), F32), jnp.zeros((dh, BS), F32))
                 for _ in range(nh))
    last = i // grp
    carries = lax.fori_loop(0, last, body, init)

    key = lax.broadcasted_iota(jnp.int32, (BS, BS), 0)
    qry = lax.broadcasted_iota(jnp.int32, (BS, BS), 1)
    causal_bias = jnp.where(key <= qry, 0.0, -jnp.inf)

    def last_bias(hh, j):
        return jnp.where(j == i, causal_bias, block_bias(hh, j))

    carries = run_parts(last, carries, last_bias, [0])
    for part in range(1, n_parts):
        carries = lax.cond(last * grp + part * sub <= i,
                           lambda c, part=part: run_parts(last, c, last_bias, [part]),
                           lambda c: c, carries)
    for hh in range(nh):
        m, l, acc = carries[hh]
        o_ref[0, :, hh * dh:(hh + 1) * dh] = (acc / l).T.astype(BF16)


MOBA_GROUP = 4
MOBA_SUB = 2
MOBA_HEADS_PER_STEP = 4


def _moba_prompt_call(p3, k3, v3, page_table, cache_k, layer):
    bsz, t, _ = p3.shape
    assert t % MOBA_BLOCK == 0 and MOBA_N_HEADS % MOBA_HEADS_PER_STEP == 0
    nb = t // MOBA_BLOCK
    grp = math.gcd(MOBA_GROUP, nb)
    nh = MOBA_HEADS_PER_STEP
    dh = MOBA_HEAD_DIM
    hw = nh * dh
    n_hg = MOBA_N_HEADS // nh
    bs, n_pages = page_table.shape
    n_steps = bsz * n_hg * nb
    pps = (bs * n_pages) // n_steps
    assert pps * n_steps == bs * n_pages and pps % PAGES_PER_BLOCK == 0 and n_pages % pps == 0
    nbs = n_pages // PAGES_PER_BLOCK

    def seq_of(b, h, i):
        return (((b * n_hg + h) * nb + i) * pps) // n_pages

    kern = functools.partial(_moba_prompt_kernel, nb=nb, grp=grp, nh=nh, layer=layer, pps=pps,
                             n_pages=n_pages)
    grid_spec = pltpu.PrefetchScalarGridSpec(
        num_scalar_prefetch=1,
        grid=(bsz, n_hg, nb),
        in_specs=[pl.BlockSpec((1, MOBA_BLOCK, hw), lambda b, h, i, pt: (b, i, COL_Q // hw + h)),
                  pl.BlockSpec((1, t, hw), lambda b, h, i, pt: (b, 0, h), pipeline_mode=pl.Buffered(1)),
                  pl.BlockSpec((1, t, hw), lambda b, h, i, pt: (b, 0, h), pipeline_mode=pl.Buffered(1)),
                  pl.BlockSpec(memory_space=pl.ANY)],
        out_specs=[pl.BlockSpec((1, MOBA_BLOCK, hw), lambda b, h, i, pt: (b, i, h)),
                   pl.BlockSpec((1, MOBA_N_HEADS, nbs, dh), lambda b, h, i, pt: (seq_of(b, h, i), 0, 0, 0))],
        scratch_shapes=[pltpu.VMEM((nh, nb, dh), F32),
                        pltpu.VMEM((nh, nb // grp, grp * MOBA_BLOCK, dh), BF16),
                        pltpu.VMEM((nh, nb // grp, dh, grp * MOBA_BLOCK), BF16),
                        pltpu.VMEM((nh, nb, SUBLANES, MOBA_BLOCK), F32),
                        pltpu.VMEM((2, pps, PAGE_SIZE, MOBA_N_HEADS, dh), F32),
                        pltpu.SemaphoreType.DMA((2,))],
    )
    return pl.pallas_call(
        kern,
        grid_spec=grid_spec,
        out_shape=[_sds((bsz, t, MOBA_WIDTH), BF16), _sds((bs, MOBA_N_HEADS, nbs, dh))],
        compiler_params=_cparams(("arbitrary", "arbitrary", "arbitrary")),
        name="moba_prompt",
    )(page_table, p3, k3, v3, cache_k)


SELECT_SEQS_PER_STEP = 8


def _select_kernel(q_ref, ks_ref, o_ref, *, nb):
    n_seq, rows = q_ref.shape[0], q_ref.shape[1]
    col = lax.broadcasted_iota(jnp.int32, (rows, nb), 1)
    colf = col.astype(F32)
    lane = lax.broadcasted_iota(jnp.int32, (rows, LANES), 1)
    for s in range(n_seq):
        out = jnp.zeros((rows, LANES), F32)
        for h in range(MOBA_N_HEADS):
            sl = slice(h * MOBA_HEAD_DIM, (h + 1) * MOBA_HEAD_DIM)
            kmean = ks_ref[s, h] * (1.0 / MOBA_BLOCK)
            gate = _dot_nt(q_ref[s, :, sl], kmean, precision=HIGHEST)
            sel, rank = _topk_mask(gate, nb, col)
            for r in range(MOBA_TOPK):
                idx = jnp.sum(jnp.where(sel & (rank == r), colf, 0.0), axis=1, keepdims=True)
                out = jnp.where(lane == h * 4 + r, idx, out)
        o_ref[s] = out.astype(jnp.int32)


def _select_call(p3, ksum):
    bs, rows, _ = p3.shape
    nb = ksum.shape[2]
    assert nb >= MOBA_TOPK
    ns = math.gcd(SELECT_SEQS_PER_STEP, bs)
    return pl.pallas_call(
        functools.partial(_select_kernel, nb=nb),
        grid=(bs // ns,),
        in_specs=[pl.BlockSpec((ns, rows, MOBA_WIDTH), lambda b: (b, 0, COL_Q // MOBA_WIDTH)),
                  pl.BlockSpec((ns, MOBA_N_HEADS, nb, MOBA_HEAD_DIM), lambda b: (b, 0, 0, 0))],
        out_specs=pl.BlockSpec((ns, rows, LANES), lambda b: (b, 0, 0)),
        out_shape=_sds((bs, rows, LANES), jnp.int32),
        compiler_params=_cparams(("arbitrary",)),
        name="moba_select",
    )(p3, ksum)


N_SEL_PAGES = MOBA_TOPK * PAGES_PER_BLOCK


def _moba_sample_kernel(pt_ref, sel_ref, q_ref, kn_ref, vn_ref, ck_hbm, cv_hbm, o_ref,
                        kbuf, vbuf, sem, *, layer, t_valid):
    step = pl.program_id(0)
    n_steps = pl.num_programs(0)
    t = step % t_valid
    slot = step % 2
    rows = q_ref.shape[1]
    dh = MOBA_HEAD_DIM
    scale = MOBA_HEAD_DIM ** -0.5

    def gather(st, sl, start):
        b = st // t_valid
        for h in range(MOBA_N_HEADS):
            for r in range(MOBA_TOPK):
                blk = sel_ref[(st * MOBA_N_HEADS + h) * MOBA_TOPK + r] if start else 0
                for half in range(PAGES_PER_BLOCK):
                    page = pt_ref[b, blk * PAGES_PER_BLOCK + half] if start else 0
                    n = r * PAGES_PER_BLOCK + half
                    for src, dst in ((ck_hbm, kbuf), (cv_hbm, vbuf)):
                        cp = pltpu.make_async_copy(src.at[layer, page, :, h, :], dst.at[sl, h, n],
                                                   sem.at[sl])
                        if start:
                            cp.start()
                        else:
                            cp.wait()

    @pl.when(step == 0)
    def _():
        gather(step, slot, True)

    @pl.when(step + 1 < n_steps)
    def _():
        gather(step + 1, 1 - slot, True)

    gather(step, slot, False)

    @pl.when(t == 0)
    def _():
        o_ref[...] = jnp.zeros(o_ref.shape, F32)

    u = lax.broadcasted_iota(jnp.int32, (rows, dh), 0)
    own_ok = (u <= t) & (u < t_valid)
    is_t = u == t
    raw = []
    for h in range(MOBA_N_HEADS):
        sl = slice(h * dh, (h + 1) * dh)
        q_row = jnp.sum(jnp.where(is_t, q_ref[0, :, sl], 0.0), axis=0, keepdims=True)
        q_rep = jnp.broadcast_to(q_row, (dh, dh)).astype(BF16)
        raw.append((_dot_nt(kn_ref[0, :, sl].astype(BF16), q_rep),
                    [_dot_nt(kbuf[slot, h, n].astype(BF16), q_rep) for n in range(N_SEL_PAGES)]))
    for h in range(MOBA_N_HEADS):
        sl = slice(h * dh, (h + 1) * dh)
        s_own = jnp.where(own_ok, raw[h][0] * scale, NEG_BIG)
        s_parts = [sp * scale for sp in raw[h][1]]
        m = jnp.max(s_own, axis=0, keepdims=True)
        for sp in s_parts:
            m = jnp.maximum(m, jnp.max(sp, axis=0, keepdims=True))
        p_own = jnp.exp(s_own - m)
        l = jnp.sum(p_own, axis=0, keepdims=True)
        acc = jnp.sum(p_own * vn_ref[0, :, sl], axis=0, keepdims=True)
        for n, sp in enumerate(s_parts):
            pp = jnp.exp(sp - m)
            l = l + jnp.sum(pp, axis=0, keepdims=True)
            acc = acc + jnp.sum(pp * vbuf[slot, h, n], axis=0, keepdims=True)
        o_ref[0, :, sl] = jnp.where(is_t, acc / l, o_ref[0, :, sl])


def _moba_sample_call(page_table, sel, p3, k3, v3, cache_k, cache_v, layer, *, t_valid):
    bs, rows, _ = p3.shape
    W = MOBA_WIDTH

    def new_spec(col):
        return pl.BlockSpec((1, rows, W), lambda s, pt, sl: (s // t_valid, 0, col // W))

    buf_shape = (2, MOBA_N_HEADS, N_SEL_PAGES, PAGE_SIZE, MOBA_HEAD_DIM)
    grid_spec = pltpu.PrefetchScalarGridSpec(
        num_scalar_prefetch=2,
        grid=(bs * t_valid,),
        in_specs=[new_spec(COL_Q), new_spec(0), new_spec(0),
                  pl.BlockSpec(memory_space=pl.ANY), pl.BlockSpec(memory_space=pl.ANY)],
        out_specs=pl.BlockSpec((1, rows, W), lambda s, pt, sl: (s // t_valid, 0, 0)),
        scratch_shapes=[pltpu.VMEM(buf_shape, F32), pltpu.VMEM(buf_shape, F32),
                        pltpu.SemaphoreType.DMA((2,))],
    )
    return pl.pallas_call(
        functools.partial(_moba_sample_kernel, layer=layer, t_valid=t_valid),
        grid_spec=grid_spec,
        out_shape=_sds((bs, rows, W)),
        compiler_params=_cparams(("arbitrary",)),
        name="moba_sample",
    )(page_table, sel.reshape(-1), p3, k3, v3, cache_k, cache_v)


def _merge_kernel(ys_ref, ym_ref, yl_ref, g0_ref, g1_ref, g2_ref, x_ref, gt_ref, gp_ref,
                  ws_ref, wm_ref, wl_ref, wo_ref, o_ref):
    merged = (jax.nn.sigmoid(g0_ref[...]) * _dot(ys_ref[...].astype(BF16), ws_ref[...])
              + jax.nn.sigmoid(g1_ref[...]) * _dot(ym_ref[...].astype(BF16), wm_ref[...])
              + jax.nn.sigmoid(g2_ref[...]) * _dot(yl_ref[...].astype(BF16), wl_ref[...]))
    mix = _dot(merged.astype(BF16), wo_ref[...])
    o_ref[...] = x_ref[...] + gt_ref[0] * (_rms(mix) * gp_ref[...])


def _merge_call(ys, ym, yl, p2, x2, mod, gp, ws, wm, wl, wo, *, tm, tiles_per_mod):
    rows, d = x2.shape
    row_spec = pl.BlockSpec((tm, d), lambda i: (i, 0))
    w_spec = pl.BlockSpec((d, d), lambda i: (0, 0), pipeline_mode=pl.Buffered(1))

    def gate_spec(n):
        return pl.BlockSpec((tm, d), lambda i: (i, COL_GATE // d + n))

    return pl.pallas_call(
        _merge_kernel,
        grid=(rows // tm,),
        in_specs=[row_spec, row_spec, row_spec, gate_spec(0), gate_spec(1), gate_spec(2), row_spec,
                  _mod_spec(mod, tiles_per_mod, MOD_GT1, d, 1),
                  pl.BlockSpec((1, d), lambda i: (0, 0)),
                  w_spec, w_spec, w_spec, w_spec],
        out_specs=row_spec,
        out_shape=_sds((rows, d)),
        compiler_params=_cparams(("arbitrary",)),
        name="merge",
    )(ys, ym, yl, p2, p2, p2, x2, mod, gp, ws, wm, wl, wo)


def _mlp_kernel(x_ref, sc_ref, sh_ref, gt_ref, g1_ref, g2_ref, wu_ref, wd_ref, o_ref):
    x = x_ref[...]
    h2 = (_rms(x) * g1_ref[...]) * (1.0 + sc_ref[0]) + sh_ref[0]
    up = jnp.maximum(_dot(h2.astype(BF16), wu_ref[...]), 0.0)
    f = _dot((up * up).astype(BF16), wd_ref[...])
    o_ref[...] = x + gt_ref[0] * (_rms(f) * g2_ref[...])


def _mlp_call(x2, mod, g1, g2, wu, wd, *, tm, tiles_per_mod):
    rows, d = x2.shape
    hid = wu.shape[1]
    row_spec = pl.BlockSpec((tm, d), lambda i: (i, 0))
    vec_spec = pl.BlockSpec((1, d), lambda i: (0, 0))
    return pl.pallas_call(
        _mlp_kernel,
        grid=(rows // tm,),
        in_specs=[row_spec,
                  _mod_spec(mod, tiles_per_mod, MOD_SC2, d, 1),
                  _mod_spec(mod, tiles_per_mod, MOD_SH2, d, 1),
                  _mod_spec(mod, tiles_per_mod, MOD_GT2, d, 1),
                  vec_spec, vec_spec,
                  pl.BlockSpec((d, hid), lambda i: (0, 0), pipeline_mode=pl.Buffered(1)),
                  pl.BlockSpec((hid, d), lambda i: (0, 0), pipeline_mode=pl.Buffered(1))],
        out_specs=row_spec,
        out_shape=_sds((rows, d)),
        compiler_params=_cparams(("arbitrary",)),
        name="mlp",
    )(x2, mod, mod, mod, g1, g2, wu, wd)


def _block_diag_tiles(w, per_tile):
    nblk, d, _ = w.shape
    wt = w.reshape(nblk // per_tile, per_tile, d, d)
    eye = jnp.eye(per_tile, dtype=w.dtype)
    bd = wt[:, :, :, None, :] * eye[None, :, None, :, None]
    return bd.reshape(nblk // per_tile, per_tile * d, per_tile * d)


def _layer_weights(l, w_in, w_branch_ssd, w_branch_moba, w_branch_lru, w_out, w_up, w_down,
                   lru_wa, lru_wx):
    o = [0]
    for s in (SSD_D_INNER, SSD_CONV_DIM, SSD_N_HEADS, MOBA_WIDTH, MOBA_WIDTH, MOBA_WIDTH,
              LRU_WIDTH, LRU_WIDTH, N_BRANCH * D_MODEL):
        o.append(o[-1] + s)
    w = w_in[l]
    seg = lambda n: w[:, o[n]:o[n + 1]]
    w_all = jnp.concatenate([seg(1), seg(0), seg(3), seg(6), seg(7), seg(8), seg(4), seg(5)],
                            axis=1).astype(BF16)
    wdt = jnp.pad(seg(2), ((0, 0), (0, DT_PAD - SSD_N_HEADS))).astype(BF16)
    per_tile = 256 // LRU_BLOCK
    return dict(
        w_all=w_all, wdt=wdt,
        ws=w_branch_ssd[l].astype(BF16), wm=w_branch_moba[l].astype(BF16),
        wl=w_branch_lru[l].astype(BF16), wo=w_out[l].astype(BF16),
        wu=w_up[l].astype(BF16), wd=w_down[l].astype(BF16),
        wa_bd=_block_diag_tiles(lru_wa[l], per_tile).astype(BF16),
        wx_bd=_block_diag_tiles(lru_wx[l], per_tile).astype(BF16),
    )


def _row(v):
    return v.reshape(1, -1)


def _pad_lanes(v, width):
    return jnp.pad(v, (0, width - v.shape[0])).reshape(1, width)


def _halo(buf):
    return jnp.pad(buf, ((0, 0), (CONV_HALO - buf.shape[1], 0), (0, 0)))


def _group_layer(x3, mods, wts, prm, ssd_buf, ssd_h, ssd_h_layer, lru_buf, lru_h, moba_fn, *, t_valid,
                 tm_proj, tm_dense, ssd_rows, ssd_seqs, lru_tm, lru_seqs, rows_per_mod):
    bsz, t_pad, d = x3.shape
    rows = bsz * t_pad
    x2 = x3.reshape(rows, d)
    tiles_per_mod = rows_per_mod // tm_dense
    p2, dt2, k2, v2 = _inproj_call(x2, mods, _row(prm['g_pre_mix']), wts['w_all'], wts['wdt'],
                                   tm=tm_proj, tiles_per_mod=rows_per_mod // tm_proj)
    p3 = p2.reshape(bsz, t_pad, P_WIDTH)
    dt3 = dt2.reshape(bsz, t_pad, DT_PAD)
    k3 = k2.reshape(bsz, t_pad, MOBA_WIDTH)
    v3 = v2.reshape(bsz, t_pad, MOBA_WIDTH)

    y_ssd, ssd_h_new = _ssd_call(
        p3, dt3, _halo(ssd_buf),
        ssd_h.reshape(ssd_h.shape[0], bsz, SSD_N_HEADS * SSD_HEAD_DIM, SSD_D_STATE), ssd_h_layer,
        prm['ssd_conv_w'], _row(prm['ssd_conv_b']), _pad_lanes(prm['ssd_dt_bias'], DT_PAD),
        _pad_lanes(prm['ssd_a_log'], DT_PAD), _row(jnp.repeat(prm['ssd_d'], SSD_HEAD_DIM)),
        _row(prm['ssd_norm_g']), rows_in=ssd_rows, t_valid=t_valid, seqs=ssd_seqs)

    y_lru, lru_h_new = _lru_call(
        p3, _halo(lru_buf), lru_h.reshape(bsz, 1, LRU_WIDTH), prm['lru_conv_w'],
        _row(prm['lru_conv_b']), wts['wa_bd'], wts['wx_bd'], _row(prm['lru_ba']),
        _row(prm['lru_bx']), _row(prm['lru_lambda']), tm=lru_tm, t_valid=t_valid, seqs=lru_seqs)

    y_moba = moba_fn(p3, k3, v3)

    x1 = _merge_call(y_ssd.reshape(rows, d), y_moba.reshape(rows, d), y_lru.reshape(rows, d), p2, x2,
                     mods, _row(prm['g_post_mix']), wts['ws'], wts['wm'], wts['wl'], wts['wo'],
                     tm=tm_dense, tiles_per_mod=tiles_per_mod)
    x_out = _mlp_call(x1, mods, _row(prm['g_pre_mlp']), _row(prm['g_post_mlp']),
                      wts['wu'], wts['wd'], tm=tm_dense, tiles_per_mod=tiles_per_mod)

    lo = t_valid - (SSD_CONV - 1)
    k_new = k3[:, :t_valid].reshape(bsz, t_valid, MOBA_N_HEADS, MOBA_HEAD_DIM)
    v_new = v3[:, :t_valid].reshape(bsz, t_valid, MOBA_N_HEADS, MOBA_HEAD_DIM)
    ssd_buf_new = p3[:, lo:t_valid, COL_XBC:COL_XBC + SSD_CONV_DIM]
    lru_buf_new = p3[:, lo:t_valid, COL_XR:COL_XR + LRU_WIDTH]
    return (x_out.reshape(bsz, t_pad, d), k_new, v_new, ssd_buf_new,
            ssd_h_new.reshape(bsz, SSD_N_HEADS, SSD_HEAD_DIM, SSD_D_STATE), lru_buf_new,
            lru_h_new.reshape(bsz, LRU_WIDTH))


def kernel(x_prompt, x_sample, c_prompt, c_sample, cache_k, cache_v, page_table, state_ssm, state_ssm_conv, state_lru, state_lru_conv, w_mod, b_mod, g_pre_mix, g_post_mix, g_pre_mlp, g_post_mlp, w_in, ssd_conv_w, ssd_conv_b, ssd_dt_bias, ssd_a_log, ssd_d, ssd_norm_g, lru_conv_w, lru_conv_b, lru_wa, lru_ba, lru_wx, lru_bx, lru_lambda, w_branch_ssd, w_branch_moba, w_branch_lru, w_out, w_up, w_down):
    bp, tp, d = x_prompt.shape
    bs, ts, _ = x_sample.shape
    depth = w_in.shape[0]
    n_pool = cache_k.shape[1]
    assert ts <= SAMPLE_T_PAD and ts >= SSD_CONV - 1
    assert (page_table.shape[1] * PAGE_SIZE) % MOBA_BLOCK == 0
    assert tp % SSD_CHUNK == 0 and tp % MOBA_BLOCK == 0

    tm_p = min(1024, tp)
    tm_dense_p = min(512, tp)
    lru_tm_p = min(512, tp)

    n_c = bp + bs
    n_c_pad = -(-n_c // SUBLANES) * SUBLANES
    c_all = jnp.pad(jnp.concatenate([c_prompt, c_sample], axis=0), ((0, n_c_pad - n_c), (0, 0)))
    mod_all = _mod_call(c_all, w_mod, b_mod)

    xs_pad = jnp.pad(x_sample, ((0, 0), (0, SAMPLE_T_PAD - ts), (0, 0)))

    zero_ssd_buf = jnp.zeros((bp, SSD_CONV - 1, SSD_CONV_DIM), F32)
    zero_ssd_h = jnp.zeros((1, bp, SSD_N_HEADS, SSD_HEAD_DIM, SSD_D_STATE), F32)
    zero_lru_buf = jnp.zeros((bp, LRU_CONV - 1, LRU_WIDTH), F32)
    zero_lru_h = jnp.zeros((bp, LRU_WIDTH), F32)

    outs_p = [[] for _ in range(6)]
    outs_s = [[] for _ in range(6)]
    xp, xs = x_prompt, xs_pad
    for l in range(depth):
        prm = {
            'g_pre_mix': g_pre_mix[l], 'g_post_mix': g_post_mix[l], 'g_pre_mlp': g_pre_mlp[l],
            'g_post_mlp': g_post_mlp[l], 'ssd_conv_w': ssd_conv_w[l], 'ssd_conv_b': ssd_conv_b[l],
            'ssd_dt_bias': ssd_dt_bias[l], 'ssd_a_log': ssd_a_log[l], 'ssd_d': ssd_d[l],
            'ssd_norm_g': ssd_norm_g[l], 'lru_conv_w': lru_conv_w[l], 'lru_conv_b': lru_conv_b[l],
            'lru_ba': lru_ba[l].reshape(-1), 'lru_bx': lru_bx[l].reshape(-1),
            'lru_lambda': lru_lambda[l],
        }
        wts = _layer_weights(l, w_in, w_branch_ssd, w_branch_moba, w_branch_lru, w_out, w_up,
                             w_down, lru_wa, lru_wx)
        mod_l = mod_all[l]
        mods_p = mod_l[:bp].reshape(bp, 1, 6 * d)
        mods_s = jnp.repeat(mod_l[bp:bp + bs], SAMPLE_T_PAD, axis=0).reshape(1, bs * SAMPLE_T_PAD, 6 * d)

        ksum_box = []

        def moba_prompt(p3, k3, v3, l=l, ksum_box=ksum_box):
            y, ksum = _moba_prompt_call(p3, k3, v3, page_table, cache_k, l)
            ksum_box.append(ksum)
            return y

        res_p = _group_layer(xp, mods_p, wts, prm, zero_ssd_buf, zero_ssd_h, 0, zero_lru_buf, zero_lru_h,
                             moba_prompt, t_valid=tp, tm_proj=tm_p, tm_dense=tm_dense_p,
                             ssd_rows=SSD_CHUNK, ssd_seqs=1, lru_tm=lru_tm_p, lru_seqs=1,
                             rows_per_mod=tp)
        xp = res_p[0]
        for n in range(6):
            outs_p[n].append(res_p[n + 1])

        def moba_sample(p3, k3, v3, l=l, ksum_box=ksum_box):
            sel = _select_call(p3, ksum_box[0])
            sel = sel[:, :ts, :MOBA_N_HEADS * 4].reshape(bs, ts, MOBA_N_HEADS, 4)[..., :MOBA_TOPK]
            return _moba_sample_call(page_table, sel, p3, k3, v3, cache_k, cache_v, l, t_valid=ts)

        res_s = _group_layer(xs, mods_s, wts, prm, state_ssm_conv[l], state_ssm, l, state_lru_conv[l],
                             state_lru[l], moba_sample, t_valid=ts, tm_proj=bs * SAMPLE_T_PAD,
                             tm_dense=bs * SAMPLE_T_PAD, ssd_rows=SAMPLE_T_PAD, ssd_seqs=2,
                             lru_tm=SAMPLE_T_PAD, lru_seqs=4,
                             rows_per_mod=bs * SAMPLE_T_PAD)
        xs = res_s[0]
        for n in range(6):
            outs_s[n].append(res_s[n + 1])

    st = lambda lst: jnp.stack(lst)
    kp, vp, ssmc_p, ssm_p, lruc_p, lru_p = outs_p
    ks, vs, ssmc_s, ssm_s, lruc_s, lru_s = outs_s
    return (xp, xs[:, :ts],
            st(kp), st(vp), st(ks), st(vs),
            st(ssm_p), st(ssmc_p), st(lru_p), st(lruc_p),
            st(ssm_s), st(ssmc_s), st(lru_s), st(lruc_s))
```

```python
import functools
import math

import jax
import jax.numpy as jnp
from jax import lax
from jax.experimental import pallas as pl
from jax.experimental.pallas import tpu as pltpu

F32 = jnp.float32
BF16 = jnp.bfloat16
HIGHEST = lax.Precision.HIGHEST

D_MODEL = 1024
PAGE_SIZE = 128
SSD_D_INNER = D_MODEL
SSD_HEAD_DIM = 64
SSD_N_HEADS = SSD_D_INNER // SSD_HEAD_DIM
SSD_N_GROUPS = 4
SSD_D_STATE = 128
SSD_CONV = 4
SSD_CHUNK = 128
SSD_CONV_DIM = SSD_D_INNER + 2 * SSD_N_GROUPS * SSD_D_STATE
MOBA_N_HEADS = 8
MOBA_HEAD_DIM = 128
MOBA_WIDTH = MOBA_N_HEADS * MOBA_HEAD_DIM
MOBA_BLOCK = 256
MOBA_TOPK = 3
LRU_WIDTH = D_MODEL
LRU_N_BLOCKS = 16
LRU_BLOCK = LRU_WIDTH // LRU_N_BLOCKS
LRU_CONV = 4
LRU_C = 8.0
N_BRANCH = 3
MLP_HIDDEN = 4 * D_MODEL
RMS_EPS = 1e-6

SUBLANES = 8
LANES = 128
VMEM_LIMIT_BYTES = 56 * 1024 * 1024

SAMPLE_T_PAD = SUBLANES
CONV_HALO = SUBLANES
NEG_BIG = -1e30
LOG2E = math.log2(math.e)

COL_XBC = 0
COL_Z = COL_XBC + SSD_CONV_DIM
COL_Q = COL_Z + SSD_D_INNER
COL_XR = COL_Q + MOBA_WIDTH
COL_GR = COL_XR + LRU_WIDTH
COL_GATE = COL_GR + LRU_WIDTH
P_WIDTH = COL_GATE + N_BRANCH * D_MODEL
PROJ_TN = 1024
P_TILES = P_WIDTH // PROJ_TN
DT_PAD = LANES


def _cparams(sem):
    return pltpu.CompilerParams(dimension_semantics=sem, vmem_limit_bytes=VMEM_LIMIT_BYTES)


def _sds(shape, dtype=F32):
    return jax.ShapeDtypeStruct(shape, dtype)


def _silu(x):
    return x * jax.nn.sigmoid(x)


def _softplus(x):
    return jnp.maximum(x, 0.0) + jnp.log1p(jnp.exp(-jnp.abs(x)))


def _rms(x):
    return x * lax.rsqrt(jnp.mean(x * x, axis=-1, keepdims=True) + RMS_EPS)


def _dot(a, b):
    return jnp.dot(a, b, preferred_element_type=F32)


def _dot_nt(a, b, precision=None):
    return lax.dot_general(a, b, (((1,), (1,)), ((), ())), precision=precision,
                           preferred_element_type=F32)


def _mod_kernel(c_ref, w_ref, b_ref, o_ref):
    s = _silu(c_ref[...]).astype(BF16)
    o_ref[0] = _dot(s, w_ref[0].astype(BF16)) + b_ref[0]


def _mod_call(c_all, w_mod, b_mod):
    depth, d, n = w_mod.shape
    rows = c_all.shape[0]
    tn = 1536
    return pl.pallas_call(
        _mod_kernel,
        grid=(depth, n // tn),
        in_specs=[pl.BlockSpec((rows, d), lambda l, j: (0, 0)),
                  pl.BlockSpec((1, d, tn), lambda l, j: (l, 0, j)),
                  pl.BlockSpec((1, 1, tn), lambda l, j: (l, 0, j))],
        out_specs=pl.BlockSpec((1, rows, tn), lambda l, j: (l, 0, j)),
        out_shape=_sds((depth, rows, n)),
        compiler_params=_cparams(("arbitrary", "arbitrary")),
        name="mod",
    )(c_all, w_mod, b_mod.reshape(depth, 1, n))


def _inproj_kernel(x_ref, sc_ref, sh_ref, g_ref, w_ref, wdt_ref, p_ref, dt_ref, k_ref, v_ref, h_scr):
    j = pl.program_id(1)

    @pl.when(j == 0)
    def _():
        y = _rms(x_ref[...]) * g_ref[...]
        hb = (y * (1.0 + sc_ref[0]) + sh_ref[0]).astype(BF16)
        h_scr[...] = hb
        dt_ref[...] = _dot(hb, wdt_ref[...])

    @pl.when(j < P_TILES)
    def _():
        p_ref[...] = _dot(h_scr[...], w_ref[...])

    @pl.when(j == P_TILES)
    def _():
        k_ref[...] = _dot(h_scr[...], w_ref[...])

    @pl.when(j == P_TILES + 1)
    def _():
        v_ref[...] = _dot(h_scr[...], w_ref[...])


MOD_SH1, MOD_SC1, MOD_GT1, MOD_SH2, MOD_SC2, MOD_GT2 = range(6)


def _mod_spec(mod, tiles_per_mod, which, d, n_grid):
    shape = (1, mod.shape[1], d)
    if n_grid == 2:
        return pl.BlockSpec(shape, lambda i, j: (i // tiles_per_mod, 0, which))
    return pl.BlockSpec(shape, lambda i: (i // tiles_per_mod, 0, which))


def _inproj_call(x2, mod, g, w_all, wdt, *, tm, tiles_per_mod):
    rows, d = x2.shape
    assert PROJ_TN == MOBA_WIDTH and w_all.shape[1] == P_WIDTH + 2 * MOBA_WIDTH
    kv_spec = pl.BlockSpec((tm, MOBA_WIDTH), lambda i, j: (i, 0))
    return pl.pallas_call(
        _inproj_kernel,
        grid=(rows // tm, P_TILES + 2),
        in_specs=[pl.BlockSpec((tm, d), lambda i, j: (i, 0)),
                  _mod_spec(mod, tiles_per_mod, MOD_SC1, d, 2),
                  _mod_spec(mod, tiles_per_mod, MOD_SH1, d, 2),
                  pl.BlockSpec((1, d), lambda i, j: (0, 0)),
                  pl.BlockSpec((d, PROJ_TN), lambda i, j: (0, j)),
                  pl.BlockSpec((d, DT_PAD), lambda i, j: (0, 0))],
        out_specs=[pl.BlockSpec((tm, PROJ_TN), lambda i, j: (i, jnp.minimum(j, P_TILES - 1))),
                   pl.BlockSpec((tm, DT_PAD), lambda i, j: (i, 0)),
                   kv_spec, kv_spec],
        out_shape=[_sds((rows, P_WIDTH)), _sds((rows, DT_PAD)),
                   _sds((rows, MOBA_WIDTH)), _sds((rows, MOBA_WIDTH))],
        scratch_shapes=[pltpu.VMEM((tm, d), BF16)],
        compiler_params=_cparams(("arbitrary", "arbitrary")),
        name="inproj",
    )(x2, mod, mod, g, w_all, wdt)


def _conv_tile(prev_scr, x, buf_ref, w_ref, b_ref, tile_idx):
    rows, width = x.shape

    @pl.when(tile_idx == 0)
    def _():
        prev_scr[...] = buf_ref[0]

    prev = prev_scr[...]
    prev_scr[...] = x[rows - CONV_HALO:, :]
    n_groups = rows // CONV_HALO
    x3 = jnp.concatenate([prev, x], axis=0).reshape(n_groups + 1, CONV_HALO, width)
    row = lax.broadcasted_iota(jnp.int32, (1, CONV_HALO, width), 1)
    acc = b_ref[...] + w_ref[SSD_CONV - 1:SSD_CONV, :] * x
    for k in range(1, SSD_CONV):
        r3 = pltpu.roll(x3, k, 1)
        shifted = jnp.where(row >= k, r3[1:], r3[:-1]).reshape(rows, width)
        acc = acc + w_ref[SSD_CONV - 1 - k:SSD_CONV - k, :] * shifted
    return acc


def _ssd_kernel(xbc_ref, z_ref, dt_ref, buf_ref, h0_ref, cw_ref, cb_ref, dtb_ref, alog_ref,
                dsk_ref, ng_ref, y_ref, h_ref, xp_scr, dt_scr, *, rows_in, t_valid, n_seq):
    L = SSD_CHUNK
    N = SSD_D_STATE
    P2 = 2 * SSD_HEAD_DIM
    c = pl.program_id(1)

    @pl.when(c == 0)
    def _():
        h_ref[...] = h0_ref[...]

    lane = lax.broadcasted_iota(jnp.int32, (L, LANES), 1)
    row = lax.broadcasted_iota(jnp.int32, (L, LANES), 0)
    ri = lax.broadcasted_iota(jnp.int32, (L, L), 0)
    ci = lax.broadcasted_iota(jnp.int32, (L, L), 1)
    causal = ri >= ci
    left = lax.broadcasted_iota(jnp.int32, (L, P2), 1) < SSD_HEAD_DIM
    left_rows = lax.broadcasted_iota(jnp.int32, (P2, N), 0) < SSD_HEAD_DIM
    a = jnp.where(lane < SSD_N_HEADS, -jnp.exp(alog_ref[...]), 0.0)

    ctx = []
    for s in range(n_seq):
        xc = _silu(_conv_tile(xp_scr.at[s], xbc_ref[s], buf_ref.at[pl.ds(s, 1)], cw_ref, cb_ref, c))
        if rows_in < L:
            xc = jnp.concatenate([xc, jnp.zeros((L - rows_in, SSD_CONV_DIM), F32)], axis=0)
            dt_scr[s] = jnp.zeros((L, LANES), F32)
        dt_scr[s, 0:rows_in, :] = dt_ref[s]
        dtv = _softplus(dt_scr[s] + dtb_ref[...])
        dtv = jnp.where((row + c * L < t_valid) & (lane < SSD_N_HEADS), dtv, 0.0)
        cum = jnp.dot(causal.astype(F32), dtv * a, precision=HIGHEST, preferred_element_type=F32)
        cum_t = cum.T
        cum_last = cum[L - 1:L, :]
        ctx.append(dict(
            xs=xc[:, :SSD_D_INNER],
            bm=xc[:, SSD_D_INNER:SSD_D_INNER + SSD_N_GROUPS * N],
            cm=xc[:, SSD_D_INNER + SSD_N_GROUPS * N:],
            cum=cum, cum_t=cum_t, dtv_t=dtv.T, ecum=jnp.exp(cum),
            wgt=jnp.exp(cum_last - cum) * dtv,
            cdec_t=jnp.exp(cum_t[:, L - 1:L]),
            y_tiles=[]))

    for j in range(SSD_N_HEADS // 2):
        g = j // (SSD_N_HEADS // SSD_N_GROUPS // 2)
        h_a, h_b = 2 * j, 2 * j + 1
        for s in range(n_seq):
            k = ctx[s]
            bm_g = k['bm'][:, g * N:(g + 1) * N]
            cm_g = k['cm'][:, g * N:(g + 1) * N].astype(BF16)
            cb = _dot_nt(cm_g, bm_g.astype(BF16))
            xs_p = k['xs'][:, j * P2:(j + 1) * P2]
            m_parts = []
            for hh in (h_a, h_b):
                seg = k['cum'][:, hh:hh + 1] - k['cum_t'][hh:hh + 1, :]
                dec = jnp.exp(jnp.where(causal, seg, -jnp.inf))
                m_parts.append((cb * dec * k['dtv_t'][hh:hh + 1, :]).astype(BF16))
            m_pair = jnp.concatenate(m_parts, axis=1)
            x_bd = jnp.concatenate([jnp.where(left, xs_p, 0.0), jnp.where(left, 0.0, xs_p)],
                                   axis=0).astype(BF16)
            y_diag = _dot(m_pair, x_bd)
            h_prev = h_ref[s, j * P2:(j + 1) * P2, :]
            y_off = _dot_nt(cm_g, h_prev.astype(BF16))
            e_sel = jnp.where(left, k['ecum'][:, h_a:h_a + 1], k['ecum'][:, h_b:h_b + 1])
            k['y_tiles'].append(y_diag + y_off * e_sel + dsk_ref[:, j * P2:(j + 1) * P2] * xs_p)
            w_sel = jnp.where(left, k['wgt'][:, h_a:h_a + 1], k['wgt'][:, h_b:h_b + 1])
            xw_t = (xs_p * w_sel).T.astype(BF16)
            st = _dot(xw_t, bm_g.astype(BF16))
            d_sel = jnp.where(left_rows, k['cdec_t'][h_a:h_a + 1, :], k['cdec_t'][h_b:h_b + 1, :])
            h_ref[s, j * P2:(j + 1) * P2, :] = h_prev * d_sel + st

    gw = SSD_D_INNER // SSD_N_GROUPS
    tiles_per_group = gw // P2
    for s in range(n_seq):
        zs = _silu(z_ref[s])
        for g in range(SSD_N_GROUPS):
            yg = jnp.concatenate(ctx[s]['y_tiles'][g * tiles_per_group:(g + 1) * tiles_per_group], axis=1)
            yg = yg[0:rows_in, :] * zs[:, g * gw:(g + 1) * gw]
            y_ref[s, :, g * gw:(g + 1) * gw] = (_rms(yg) * ng_ref[:, g * gw:(g + 1) * gw]).astype(BF16)


def _ssd_call(p3, dt3, buf8, h0, h0_layer, cw, cb, dtb, alog, dsk, ng, *, rows_in, t_valid, seqs):
    bsz, t_pad, _ = p3.shape
    nc = t_pad // rows_in
    L = SSD_CHUNK
    hp = SSD_N_HEADS * SSD_HEAD_DIM
    ns = math.gcd(seqs, bsz)
    const2 = lambda b, c: (0, 0)
    kern = functools.partial(_ssd_kernel, rows_in=rows_in, t_valid=t_valid, n_seq=ns)
    return pl.pallas_call(
        kern,
        grid=(bsz // ns, nc),
        in_specs=[pl.BlockSpec((ns, rows_in, SSD_CONV_DIM), lambda b, c: (b, c, COL_XBC // SSD_CONV_DIM)),
                  pl.BlockSpec((ns, rows_in, SSD_D_INNER), lambda b, c: (b, c, COL_Z // SSD_D_INNER)),
                  pl.BlockSpec((ns, rows_in, DT_PAD), lambda b, c: (b, c, 0)),
                  pl.BlockSpec((ns, CONV_HALO, SSD_CONV_DIM), lambda b, c: (b, 0, 0)),
                  pl.BlockSpec((None, ns, hp, SSD_D_STATE), lambda b, c: (h0_layer, b, 0, 0)),
                  pl.BlockSpec((SSD_CONV, SSD_CONV_DIM), const2),
                  pl.BlockSpec((1, SSD_CONV_DIM), const2),
                  pl.BlockSpec((1, DT_PAD), const2),
                  pl.BlockSpec((1, DT_PAD), const2),
                  pl.BlockSpec((1, SSD_D_INNER), const2),
                  pl.BlockSpec((1, SSD_D_INNER), const2)],
        out_specs=[pl.BlockSpec((ns, rows_in, SSD_D_INNER), lambda b, c: (b, c, 0)),
                   pl.BlockSpec((ns, hp, SSD_D_STATE), lambda b, c: (b, 0, 0))],
        out_shape=[_sds((bsz, t_pad, SSD_D_INNER), BF16), _sds((bsz, hp, SSD_D_STATE))],
        scratch_shapes=[pltpu.VMEM((ns, CONV_HALO, SSD_CONV_DIM), F32),
                        pltpu.VMEM((ns, L, DT_PAD), F32)],
        compiler_params=_cparams(("arbitrary", "arbitrary")),
        name="ssd",
    )(p3, p3, dt3, buf8, h0, cw, cb, dtb, alog, dsk, ng)


def _lru_kernel(xr_ref, gr_ref, buf_ref, h0_ref, cw_ref, cb_ref, wa_ref, wx_ref, ba_ref, bx_ref,
                lam_ref, y_ref, h_ref, xp_scr, *, tm, t_valid, t_pad):
    for s in range(xr_ref.shape[0]):
        one = pl.ds(s, 1)
        _lru_tile(xr_ref.at[one], gr_ref.at[one], buf_ref.at[one], h0_ref.at[one], cw_ref, cb_ref,
                  wa_ref, wx_ref, ba_ref, bx_ref, lam_ref, y_ref.at[one], h_ref.at[one],
                  xp_scr.at[s], tm=tm, t_valid=t_valid, t_pad=t_pad)


def _lru_tile(xr_ref, gr_ref, buf_ref, h0_ref, cw_ref, cb_ref, wa_ref, wx_ref, ba_ref, bx_ref,
              lam_ref, y_ref, h_ref, xp_scr, *, tm, t_valid, t_pad):
    i = pl.program_id(1)

    @pl.when(i == 0)
    def _():
        h_ref[0] = h0_ref[0]

    xc = _conv_tile(xp_scr, xr_ref[0], buf_ref, cw_ref, cb_ref, i)
    bw = wa_ref.shape[1]
    r_parts, i_parts = [], []
    for j in range(LRU_WIDTH // bw):
        xj = xc[:, j * bw:(j + 1) * bw].astype(BF16)
        r_parts.append(_dot(xj, wa_ref[j]))
        i_parts.append(_dot(xj, wx_ref[j]))
    r = jax.nn.sigmoid(jnp.concatenate(r_parts, axis=1) + ba_ref[...])
    ig = jax.nn.sigmoid(jnp.concatenate(i_parts, axis=1) + bx_ref[...])
    log_a = (-LRU_C) * r * _softplus(-lam_ref[...])
    a_full = jnp.exp(log_a)
    b_full = jnp.sqrt(-jnp.tanh(log_a) * (a_full * a_full + 1.0)) * (ig * xc)
    gate = jax.nn.gelu(gr_ref[0], approximate=True)

    n_groups = tm // SUBLANES
    row = lax.broadcasted_iota(jnp.int32, (tm, LANES), 0)
    row_in_group = lax.broadcasted_iota(jnp.int32, (1, SUBLANES, LANES), 1)
    for s in range(LRU_WIDTH // LANES):
        sl = slice(s * LANES, (s + 1) * LANES)
        a, b = a_full[:, sl], b_full[:, sl]
        if t_valid < t_pad:
            valid = row + i * tm < t_valid
            a = jnp.where(valid, a, 1.0)
            b = jnp.where(valid, b, 0.0)
        a = a.reshape(n_groups, SUBLANES, LANES)
        b = b.reshape(n_groups, SUBLANES, LANES)
        k = 1
        while k < SUBLANES:
            keep = row_in_group >= k
            b = b + a * jnp.where(keep, pltpu.roll(b, k, 1), 0.0)
            a = a * jnp.where(keep, pltpu.roll(a, k, 1), 1.0)
            k *= 2
        carry = h_ref[0, :, sl]
        groups = []
        for g in range(n_groups):
            hg = b[g] + a[g] * carry
            groups.append(hg)
            carry = hg[SUBLANES - 1:SUBLANES, :]
        h = groups[0] if n_groups == 1 else jnp.concatenate(groups, axis=0)
        y_ref[0, :, sl] = (h * gate[:, sl]).astype(BF16)
        h_ref[0, :, sl] = carry


def _lru_call(p3, buf8, h0, cw, cb, wa_bd, wx_bd, ba, bx, lam, *, tm, t_valid, seqs):
    bsz, t_pad, _ = p3.shape
    W = LRU_WIDTH
    nb, bw, _ = wa_bd.shape
    ns = math.gcd(seqs, bsz)
    const2 = lambda b, i: (0, 0)
    const3 = lambda b, i: (0, 0, 0)
    kern = functools.partial(_lru_kernel, tm=tm, t_valid=t_valid, t_pad=t_pad)
    return pl.pallas_call(
        kern,
        grid=(bsz // ns, t_pad // tm),
        in_specs=[pl.BlockSpec((ns, tm, W), lambda b, i: (b, i, COL_XR // W)),
                  pl.BlockSpec((ns, tm, W), lambda b, i: (b, i, COL_GR // W)),
                  pl.BlockSpec((ns, CONV_HALO, W), lambda b, i: (b, 0, 0)),
                  pl.BlockSpec((ns, 1, W), lambda b, i: (b, 0, 0)),
                  pl.BlockSpec((LRU_CONV, W), const2),
                  pl.BlockSpec((1, W), const2),
                  pl.BlockSpec((nb, bw, bw), const3),
                  pl.BlockSpec((nb, bw, bw), const3),
                  pl.BlockSpec((1, W), const2),
                  pl.BlockSpec((1, W), const2),
                  pl.BlockSpec((1, W), const2)],
        out_specs=[pl.BlockSpec((ns, tm, W), lambda b, i: (b, i, 0)),
                   pl.BlockSpec((ns, 1, W), lambda b, i: (b, 0, 0))],
        out_shape=[_sds((bsz, t_pad, W), BF16), _sds((bsz, 1, W))],
        scratch_shapes=[pltpu.VMEM((ns, CONV_HALO, W), F32)],
        compiler_params=_cparams(("arbitrary", "arbitrary")),
        name="lru",
    )(p3, p3, buf8, h0, cw, cb, wa_bd, wx_bd, ba, bx, lam)


def _topk_mask(gate, n_valid, col):
    nb = gate.shape[1]
    past = col < n_valid
    gm = jnp.where(past, gate, -jnp.inf)
    rank = jnp.zeros(gate.shape, F32)
    for j in range(nb):
        gj = gm[:, j:j + 1]
        ahead = (gj > gm) | ((gj == gm) & (col > j))
        rank = rank + jnp.where(ahead, 1.0, 0.0)
    return (rank < MOBA_TOPK) & past, rank


PAGES_PER_BLOCK = MOBA_BLOCK // PAGE_SIZE


def _ksum_stream(pt_ref, ck_hbm, ks_ref, kbuf, sem, step, n_steps, *, layer, pps, n_pages):
    def copies(st, slot, lookup):
        out = []
        for u in range(pps):
            if lookup:
                p = st * pps + u
                page = pt_ref[lax.div(p, n_pages), lax.rem(p, n_pages)]
            else:
                page = 0
            out.append(pltpu.make_async_copy(ck_hbm.at[layer, page], kbuf.at[slot, u], sem.at[slot]))
        return out

    slot = lax.rem(step, 2)

    @pl.when(step == 0)
    def _():
        for c in copies(step, slot, True):
            c.start()

    @pl.when(step + 1 < n_steps)
    def _():
        for c in copies(step + 1, 1 - slot, True):
            c.start()

    for c in copies(step, slot, False):
        c.wait()

    blk0 = lax.div(lax.rem(step * pps, n_pages), PAGES_PER_BLOCK)

    @pl.when(blk0 == 0)
    def _():
        ks_ref[...] = jnp.zeros(ks_ref.shape, F32)

    sums = []
    for r in range(pps // PAGES_PER_BLOCK):
        acc = jnp.sum(kbuf[slot, r * PAGES_PER_BLOCK], axis=0)
        for u in range(1, PAGES_PER_BLOCK):
            acc = acc + jnp.sum(kbuf[slot, r * PAGES_PER_BLOCK + u], axis=0)
        sums.append(acc)
    row = lax.broadcasted_iota(jnp.int32, ks_ref.shape[2:], 0)
    for h in range(MOBA_N_HEADS):
        cur = ks_ref[0, h]
        for r, acc in enumerate(sums):
            cur = jnp.where(row == blk0 + r, acc[h:h + 1, :], cur)
        ks_ref[0, h] = cur


def _moba_prompt_kernel(pt_ref, q_ref, k_ref, v_ref, ck_hbm, o_ref, ks_ref, kmean_scr, kg_scr, vtg_scr,
                        sel_scr, kbuf, sem, *, nb, grp, nh, layer, pps, n_pages):
    i = pl.program_id(2)
    BS = MOBA_BLOCK
    GB = grp * BS
    dh = MOBA_HEAD_DIM
    scale = MOBA_HEAD_DIM ** -0.5

    step = (pl.program_id(0) * pl.num_programs(1) + pl.program_id(1)) * pl.num_programs(2) + i
    n_steps = pl.num_programs(0) * pl.num_programs(1) * pl.num_programs(2)
    _ksum_stream(pt_ref, ck_hbm, ks_ref, kbuf, sem, step, n_steps, layer=layer, pps=pps,
                 n_pages=n_pages)

    @pl.when(i == 0)
    def _():
        for hh in range(nh):
            hs = slice(hh * dh, (hh + 1) * dh)
            for j in range(nb):
                kmean_scr[hh, j:j + 1, :] = jnp.mean(k_ref[0, j * BS:(j + 1) * BS, hs], axis=0,
                                                     keepdims=True)
            for jj in range(nb // grp):
                kg_scr[hh, jj] = k_ref[0, jj * GB:(jj + 1) * GB, hs].astype(BF16)
                for u in range(grp):
                    r0 = jj * GB + u * BS
                    vtg_scr[hh, jj, :, u * BS:(u + 1) * BS] = v_ref[0, r0:r0 + BS, hs].T.astype(BF16)

    qb_t = []
    for hh in range(nh):
        q_t = q_ref[0, :, hh * dh:(hh + 1) * dh].T
        qb_t.append(q_t.astype(BF16))
        gate = jnp.dot(kmean_scr[hh], q_t, precision=HIGHEST, preferred_element_type=F32)
        blk = lax.broadcasted_iota(jnp.int32, gate.shape, 0)
        past = blk < i
        gm = jnp.where(past, gate, -jnp.inf)
        rank = jnp.zeros(gate.shape, F32)
        for j in range(nb):
            gj = gm[j:j + 1, :]
            ahead = (gj > gm) | ((gj == gm) & (blk > j))
            rank = rank + jnp.where(ahead, 1.0, 0.0)
        sel = jnp.where((rank < MOBA_TOPK) & past, 0.0, -jnp.inf)
        for j in range(nb):
            sel_scr[hh, j] = jnp.broadcast_to(sel[j:j + 1, :], (SUBLANES, BS))

    def block_bias(hh, j):
        rows = sel_scr[hh, j]
        return jnp.broadcast_to(rows[None], (BS // SUBLANES, SUBLANES, BS)).reshape(BS, BS)

    sub = min(MOBA_SUB, grp)
    n_parts = grp // sub
    SB = sub * BS

    def scores(hh, jj, part):
        return _dot(kg_scr[hh, jj, part * SB:(part + 1) * SB, :], qb_t[hh])

    def update(carry, hh, jj, part, s_raw, bias):
        m, l, acc = carry
        s = s_raw * (scale * LOG2E) + bias
        m_new = jnp.maximum(m, jnp.max(s, axis=0, keepdims=True))
        alpha = jnp.exp2(m - m_new)
        p = jnp.exp2(s - m_new)
        l_new = alpha * l + jnp.sum(p, axis=0, keepdims=True)
        v_t = vtg_scr[hh, jj, :, part * SB:(part + 1) * SB]
        acc_new = alpha * acc + _dot(v_t, p.astype(BF16))
        return m_new, l_new, acc_new

    def run_parts(jj, carries, bias_fn, parts):
        s_raw = {part: [scores(hh, jj, part) for hh in range(nh)] for part in parts}
        carries = list(carries)
        for part in parts:
            for hh in range(nh):
                bias = jnp.concatenate(
                    [bias_fn(hh, jj * grp + part * sub + u) for u in range(sub)], axis=0)
                carries[hh] = update(carries[hh], hh, jj, part, s_raw[part][hh], bias)
        return tuple(carries)

    def body(jj, carries):
        return run_parts(jj, carries, block_bias, range(n_parts))

    init = tuple((jnp.full((1, BS), NEG_BIG, F32), jnp.zeros((1, BS), F32), jnp.zeros((dh, BS), F32))
                 for _ in range(nh))
    last = i // grp
    carries = lax.fori_loop(0, last, body, init)

    key = lax.broadcasted_iota(jnp.int32, (BS, BS), 0)
    qry = lax.broadcasted_iota(jnp.int32, (BS, BS), 1)
    causal_bias = jnp.where(key <= qry, 0.0, -jnp.inf)

    def last_bias(hh, j):
        return jnp.where(j == i, causal_bias, block_bias(hh, j))

    carries = run_parts(last, carries, last_bias, [0])
    for part in range(1, n_parts):
        carries = lax.cond(last * grp + part * sub <= i,
                           lambda c, part=part: run_parts(last, c, last_bias, [part]),
                           lambda c: c, carries)
    for hh in range(nh):
        m, l, acc = carries[hh]
        o_ref[0, :, hh * dh:(hh + 1) * dh] = (acc / l).T.astype(BF16)


MOBA_GROUP = 4
MOBA_SUB = 2
MOBA_HEADS_PER_STEP = 4


def _moba_prompt_call(p3, k3, v3, page_table, cache_k, layer):
    bsz, t, _ = p3.shape
    assert t % MOBA_BLOCK == 0 and MOBA_N_HEADS % MOBA_HEADS_PER_STEP == 0
    nb = t // MOBA_BLOCK
    grp = math.gcd(MOBA_GROUP, nb)
    nh = MOBA_HEADS_PER_STEP
    dh = MOBA_HEAD_DIM
    hw = nh * dh
    n_hg = MOBA_N_HEADS // nh
    bs, n_pages = page_table.shape
    n_steps = bsz * n_hg * nb
    pps = (bs * n_pages) // n_steps
    assert pps * n_steps == bs * n_pages and pps % PAGES_PER_BLOCK == 0 and n_pages % pps == 0
    nbs = n_pages // PAGES_PER_BLOCK

    def seq_of(b, h, i):
        return (((b * n_hg + h) * nb + i) * pps) // n_pages

    kern = functools.partial(_moba_prompt_kernel, nb=nb, grp=grp, nh=nh, layer=layer, pps=pps,
                             n_pages=n_pages)
    grid_spec = pltpu.PrefetchScalarGridSpec(
        num_scalar_prefetch=1,
        grid=(bsz, n_hg, nb),
        in_specs=[pl.BlockSpec((1, MOBA_BLOCK, hw), lambda b, h, i, pt: (b, i, COL_Q // hw + h)),
                  pl.BlockSpec((1, t, hw), lambda b, h, i, pt: (b, 0, h), pipeline_mode=pl.Buffered(1)),
                  pl.BlockSpec((1, t, hw), lambda b, h, i, pt: (b, 0, h), pipeline_mode=pl.Buffered(1)),
                  pl.BlockSpec(memory_space=pl.ANY)],
        out_specs=[pl.BlockSpec((1, MOBA_BLOCK, hw), lambda b, h, i, pt: (b, i, h)),
                   pl.BlockSpec((1, MOBA_N_HEADS, nbs, dh), lambda b, h, i, pt: (seq_of(b, h, i), 0, 0, 0))],
        scratch_shapes=[pltpu.VMEM((nh, nb, dh), F32),
                        pltpu.VMEM((nh, nb // grp, grp * MOBA_BLOCK, dh), BF16),
                        pltpu.VMEM((nh, nb // grp, dh, grp * MOBA_BLOCK), BF16),
                        pltpu.VMEM((nh, nb, SUBLANES, MOBA_BLOCK), F32),
                        pltpu.VMEM((2, pps, PAGE_SIZE, MOBA_N_HEADS, dh), F32),
                        pltpu.SemaphoreType.DMA((2,))],
    )
    return pl.pallas_call(
        kern,
        grid_spec=grid_spec,
        out_shape=[_sds((bsz, t, MOBA_WIDTH), BF16), _sds((bs, MOBA_N_HEADS, nbs, dh))],
        compiler_params=_cparams(("arbitrary", "arbitrary", "arbitrary")),
        name="moba_prompt",
    )(page_table, p3, k3, v3, cache_k)


SELECT_SEQS_PER_STEP = 8


def _select_kernel(q_ref, ks_ref, o_ref, *, nb):
    n_seq, rows = q_ref.shape[0], q_ref.shape[1]
    col = lax.broadcasted_iota(jnp.int32, (rows, nb), 1)
    colf = col.astype(F32)
    lane = lax.broadcasted_iota(jnp.int32, (rows, LANES), 1)
    for s in range(n_seq):
        out = jnp.zeros((rows, LANES), F32)
        for h in range(MOBA_N_HEADS):
            sl = slice(h * MOBA_HEAD_DIM, (h + 1) * MOBA_HEAD_DIM)
            kmean = ks_ref[s, h] * (1.0 / MOBA_BLOCK)
            gate = _dot_nt(q_ref[s, :, sl], kmean, precision=HIGHEST)
            sel, rank = _topk_mask(gate, nb, col)
            for r in range(MOBA_TOPK):
                idx = jnp.sum(jnp.where(sel & (rank == r), colf, 0.0), axis=1, keepdims=True)
                out = jnp.where(lane == h * 4 + r, idx, out)
        o_ref[s] = out.astype(jnp.int32)


def _select_call(p3, ksum):
    bs, rows, _ = p3.shape
    nb = ksum.shape[2]
    assert nb >= MOBA_TOPK
    ns = math.gcd(SELECT_SEQS_PER_STEP, bs)
    return pl.pallas_call(
        functools.partial(_select_kernel, nb=nb),
        grid=(bs // ns,),
        in_specs=[pl.BlockSpec((ns, rows, MOBA_WIDTH), lambda b: (b, 0, COL_Q // MOBA_WIDTH)),
                  pl.BlockSpec((ns, MOBA_N_HEADS, nb, MOBA_HEAD_DIM), lambda b: (b, 0, 0, 0))],
        out_specs=pl.BlockSpec((ns, rows, LANES), lambda b: (b, 0, 0)),
        out_shape=_sds((bs, rows, LANES), jnp.int32),
        compiler_params=_cparams(("arbitrary",)),
        name="moba_select",
    )(p3, ksum)


N_SEL_PAGES = MOBA_TOPK * PAGES_PER_BLOCK


def _moba_sample_kernel(pt_ref, sel_ref, q_ref, kn_ref, vn_ref, ck_hbm, cv_hbm, o_ref,
                        kbuf, vbuf, sem, *, layer, t_valid):
    step = pl.program_id(0)
    n_steps = pl.num_programs(0)
    t = step % t_valid
    slot = step % 2
    rows = q_ref.shape[1]
    dh = MOBA_HEAD_DIM
    scale = MOBA_HEAD_DIM ** -0.5

    def gather(st, sl, start):
        b = st // t_valid
        for h in range(MOBA_N_HEADS):
            for r in range(MOBA_TOPK):
                blk = sel_ref[(st * MOBA_N_HEADS + h) * MOBA_TOPK + r] if start else 0
                for half in range(PAGES_PER_BLOCK):
                    page = pt_ref[b, blk * PAGES_PER_BLOCK + half] if start else 0
                    n = r * PAGES_PER_BLOCK + half
                    for queue, (src, dst) in enumerate(((ck_hbm, kbuf), (cv_hbm, vbuf))):
                        cp = pltpu.make_async_copy(src.at[layer, page, :, h, :], dst.at[sl, h, n],
                                                   sem.at[sl])
                        if start:
                            cp.start(priority=queue)
                        else:
                            cp.wait()

    @pl.when(step == 0)
    def _():
        gather(step, slot, True)

    @pl.when(step + 1 < n_steps)
    def _():
        gather(step + 1, 1 - slot, True)

    gather(step, slot, False)

    @pl.when(t == 0)
    def _():
        o_ref[...] = jnp.zeros(o_ref.shape, F32)

    u = lax.broadcasted_iota(jnp.int32, (rows, dh), 0)
    own_ok = (u <= t) & (u < t_valid)
    is_t = u == t
    raw = []
    for h in range(MOBA_N_HEADS):
        sl = slice(h * dh, (h + 1) * dh)
        q_row = jnp.sum(jnp.where(is_t, q_ref[0, :, sl], 0.0), axis=0, keepdims=True)
        q_rep = jnp.broadcast_to(q_row, (dh, dh)).astype(BF16)
        raw.append((_dot_nt(kn_ref[0, :, sl].astype(BF16), q_rep),
                    [_dot_nt(kbuf[slot, h, n].astype(BF16), q_rep) for n in range(N_SEL_PAGES)]))
    for h in range(MOBA_N_HEADS):
        sl = slice(h * dh, (h + 1) * dh)
        s_own = jnp.where(own_ok, raw[h][0] * scale, NEG_BIG)
        s_parts = [sp * scale for sp in raw[h][1]]
        m = jnp.max(s_own, axis=0, keepdims=True)
        for sp in s_parts:
            m = jnp.maximum(m, jnp.max(sp, axis=0, keepdims=True))
        p_own = jnp.exp(s_own - m)
        l = jnp.sum(p_own, axis=0, keepdims=True)
        acc = jnp.sum(p_own * vn_ref[0, :, sl], axis=0, keepdims=True)
        for n, sp in enumerate(s_parts):
            pp = jnp.exp(sp - m)
            l = l + jnp.sum(pp, axis=0, keepdims=True)
            acc = acc + jnp.sum(pp * vbuf[slot, h, n], axis=0, keepdims=True)
        o_ref[0, :, sl] = jnp.where(is_t, acc / l, o_ref[0, :, sl])


def _moba_sample_call(page_table, sel, p3, k3, v3, cache_k, cache_v, layer, *, t_valid):
    bs, rows, _ = p3.shape
    W = MOBA_WIDTH

    def new_spec(col):
        return pl.BlockSpec((1, rows, W), lambda s, pt, sl: (s // t_valid, 0, col // W))

    buf_shape = (2, MOBA_N_HEADS, N_SEL_PAGES, PAGE_SIZE, MOBA_HEAD_DIM)
    grid_spec = pltpu.PrefetchScalarGridSpec(
        num_scalar_prefetch=2,
        grid=(bs * t_valid,),
        in_specs=[new_spec(COL_Q), new_spec(0), new_spec(0),
                  pl.BlockSpec(memory_space=pl.ANY), pl.BlockSpec(memory_space=pl.ANY)],
        out_specs=pl.BlockSpec((1, rows, W), lambda s, pt, sl: (s // t_valid, 0, 0)),
        scratch_shapes=[pltpu.VMEM(buf_shape, F32), pltpu.VMEM(buf_shape, F32),
                        pltpu.SemaphoreType.DMA((2,))],
    )
    return pl.pallas_call(
        functools.partial(_moba_sample_kernel, layer=layer, t_valid=t_valid),
        grid_spec=grid_spec,
        out_shape=_sds((bs, rows, W)),
        compiler_params=_cparams(("arbitrary",)),
        name="moba_sample",
    )(page_table, sel.reshape(-1), p3, k3, v3, cache_k, cache_v)


def _merge_kernel(ys_ref, ym_ref, yl_ref, g0_ref, g1_ref, g2_ref, x_ref, gt_ref, gp_ref,
                  ws_ref, wm_ref, wl_ref, wo_ref, o_ref):
    merged = (jax.nn.sigmoid(g0_ref[...]) * _dot(ys_ref[...].astype(BF16), ws_ref[...])
              + jax.nn.sigmoid(g1_ref[...]) * _dot(ym_ref[...].astype(BF16), wm_ref[...])
              + jax.nn.sigmoid(g2_ref[...]) * _dot(yl_ref[...].astype(BF16), wl_ref[...]))
    mix = _dot(merged.astype(BF16), wo_ref[...])
    o_ref[...] = x_ref[...] + gt_ref[0] * (_rms(mix) * gp_ref[...])


def _merge_call(ys, ym, yl, p2, x2, mod, gp, ws, wm, wl, wo, *, tm, tiles_per_mod):
    rows, d = x2.shape
    row_spec = pl.BlockSpec((tm, d), lambda i: (i, 0))
    w_spec = pl.BlockSpec((d, d), lambda i: (0, 0), pipeline_mode=pl.Buffered(1))

    def gate_spec(n):
        return pl.BlockSpec((tm, d), lambda i: (i, COL_GATE // d + n))

    return pl.pallas_call(
        _merge_kernel,
        grid=(rows // tm,),
        in_specs=[row_spec, row_spec, row_spec, gate_spec(0), gate_spec(1), gate_spec(2), row_spec,
                  _mod_spec(mod, tiles_per_mod, MOD_GT1, d, 1),
                  pl.BlockSpec((1, d), lambda i: (0, 0)),
                  w_spec, w_spec, w_spec, w_spec],
        out_specs=row_spec,
        out_shape=_sds((rows, d)),
        compiler_params=_cparams(("arbitrary",)),
        name="merge",
    )(ys, ym, yl, p2, p2, p2, x2, mod, gp, ws, wm, wl, wo)


def _mlp_kernel(x_ref, sc_ref, sh_ref, gt_ref, g1_ref, g2_ref, wu_ref, wd_ref, o_ref):
    x = x_ref[...]
    h2 = (_rms(x) * g1_ref[...]) * (1.0 + sc_ref[0]) + sh_ref[0]
    up = jnp.maximum(_dot(h2.astype(BF16), wu_ref[...]), 0.0)
    f = _dot((up * up).astype(BF16), wd_ref[...])
    o_ref[...] = x + gt_ref[0] * (_rms(f) * g2_ref[...])


def _mlp_call(x2, mod, g1, g2, wu, wd, *, tm, tiles_per_mod):
    rows, d = x2.shape
    hid = wu.shape[1]
    row_spec = pl.BlockSpec((tm, d), lambda i: (i, 0))
    vec_spec = pl.BlockSpec((1, d), lambda i: (0, 0))
    return pl.pallas_call(
        _mlp_kernel,
        grid=(rows // tm,),
        in_specs=[row_spec,
                  _mod_spec(mod, tiles_per_mod, MOD_SC2, d, 1),
                  _mod_spec(mod, tiles_per_mod, MOD_SH2, d, 1),
                  _mod_spec(mod, tiles_per_mod, MOD_GT2, d, 1),
                  vec_spec, vec_spec,
                  pl.BlockSpec((d, hid), lambda i: (0, 0), pipeline_mode=pl.Buffered(1)),
                  pl.BlockSpec((hid, d), lambda i: (0, 0), pipeline_mode=pl.Buffered(1))],
        out_specs=row_spec,
        out_shape=_sds((rows, d)),
        compiler_params=_cparams(("arbitrary",)),
        name="mlp",
    )(x2, mod, mod, mod, g1, g2, wu, wd)


def _block_diag_tiles(w, per_tile):
    nblk, d, _ = w.shape
    wt = w.reshape(nblk // per_tile, per_tile, d, d)
    eye = jnp.eye(per_tile, dtype=w.dtype)
    bd = wt[:, :, :, None, :] * eye[None, :, None, :, None]
    return bd.reshape(nblk // per_tile, per_tile * d, per_tile * d)


def _layer_weights(l, w_in, w_branch_ssd, w_branch_moba, w_branch_lru, w_out, w_up, w_down,
                   lru_wa, lru_wx):
    o = [0]
    for s in (SSD_D_INNER, SSD_CONV_DIM, SSD_N_HEADS, MOBA_WIDTH, MOBA_WIDTH, MOBA_WIDTH,
              LRU_WIDTH, LRU_WIDTH, N_BRANCH * D_MODEL):
        o.append(o[-1] + s)
    w = w_in[l]
    seg = lambda n: w[:, o[n]:o[n + 1]]
    w_all = jnp.concatenate([seg(1), seg(0), seg(3), seg(6), seg(7), seg(8), seg(4), seg(5)],
                            axis=1).astype(BF16)
    wdt = jnp.pad(seg(2), ((0, 0), (0, DT_PAD - SSD_N_HEADS))).astype(BF16)
    per_tile = 256 // LRU_BLOCK
    return dict(
        w_all=w_all, wdt=wdt,
        ws=w_branch_ssd[l].astype(BF16), wm=w_branch_moba[l].astype(BF16),
        wl=w_branch_lru[l].astype(BF16), wo=w_out[l].astype(BF16),
        wu=w_up[l].astype(BF16), wd=w_down[l].astype(BF16),
        wa_bd=_block_diag_tiles(lru_wa[l], per_tile).astype(BF16),
        wx_bd=_block_diag_tiles(lru_wx[l], per_tile).astype(BF16),
    )


def _row(v):
    return v.reshape(1, -1)


def _pad_lanes(v, width):
    return jnp.pad(v, (0, width - v.shape[0])).reshape(1, width)


def _halo(buf):
    return jnp.pad(buf, ((0, 0), (CONV_HALO - buf.shape[1], 0), (0, 0)))


def _group_layer(x3, mods, wts, prm, ssd_buf, ssd_h, ssd_h_layer, lru_buf, lru_h, moba_fn, *, t_valid,
                 tm_proj, tm_dense, ssd_rows, ssd_seqs, lru_tm, lru_seqs, rows_per_mod):
    bsz, t_pad, d = x3.shape
    rows = bsz * t_pad
    x2 = x3.reshape(rows, d)
    tiles_per_mod = rows_per_mod // tm_dense
    p2, dt2, k2, v2 = _inproj_call(x2, mods, _row(prm['g_pre_mix']), wts['w_all'], wts['wdt'],
                                   tm=tm_proj, tiles_per_mod=rows_per_mod // tm_proj)
    p3 = p2.reshape(bsz, t_pad, P_WIDTH)
    dt3 = dt2.reshape(bsz, t_pad, DT_PAD)
    k3 = k2.reshape(bsz, t_pad, MOBA_WIDTH)
    v3 = v2.reshape(bsz, t_pad, MOBA_WIDTH)

    y_ssd, ssd_h_new = _ssd_call(
        p3, dt3, _halo(ssd_buf),
        ssd_h.reshape(ssd_h.shape[0], bsz, SSD_N_HEADS * SSD_HEAD_DIM, SSD_D_STATE), ssd_h_layer,
        prm['ssd_conv_w'], _row(prm['ssd_conv_b']), _pad_lanes(prm['ssd_dt_bias'], DT_PAD),
        _pad_lanes(prm['ssd_a_log'], DT_PAD), _row(jnp.repeat(prm['ssd_d'], SSD_HEAD_DIM)),
        _row(prm['ssd_norm_g']), rows_in=ssd_rows, t_valid=t_valid, seqs=ssd_seqs)

    y_lru, lru_h_new = _lru_call(
        p3, _halo(lru_buf), lru_h.reshape(bsz, 1, LRU_WIDTH), prm['lru_conv_w'],
        _row(prm['lru_conv_b']), wts['wa_bd'], wts['wx_bd'], _row(prm['lru_ba']),
        _row(prm['lru_bx']), _row(prm['lru_lambda']), tm=lru_tm, t_valid=t_valid, seqs=lru_seqs)

    y_moba = moba_fn(p3, k3, v3)

    x1 = _merge_call(y_ssd.reshape(rows, d), y_moba.reshape(rows, d), y_lru.reshape(rows, d), p2, x2,
                     mods, _row(prm['g_post_mix']), wts['ws'], wts['wm'], wts['wl'], wts['wo'],
                     tm=tm_dense, tiles_per_mod=tiles_per_mod)
    x_out = _mlp_call(x1, mods, _row(prm['g_pre_mlp']), _row(prm['g_post_mlp']),
                      wts['wu'], wts['wd'], tm=tm_dense, tiles_per_mod=tiles_per_mod)

    lo = t_valid - (SSD_CONV - 1)
    k_new = k3[:, :t_valid].reshape(bsz, t_valid, MOBA_N_HEADS, MOBA_HEAD_DIM)
    v_new = v3[:, :t_valid].reshape(bsz, t_valid, MOBA_N_HEADS, MOBA_HEAD_DIM)
    ssd_buf_new = p3[:, lo:t_valid, COL_XBC:COL_XBC + SSD_CONV_DIM]
    lru_buf_new = p3[:, lo:t_valid, COL_XR:COL_XR + LRU_WIDTH]
    return (x_out.reshape(bsz, t_pad, d), k_new, v_new, ssd_buf_new,
            ssd_h_new.reshape(bsz, SSD_N_HEADS, SSD_HEAD_DIM, SSD_D_STATE), lru_buf_new,
            lru_h_new.reshape(bsz, LRU_WIDTH))


def kernel(x_prompt, x_sample, c_prompt, c_sample, cache_k, cache_v, page_table, state_ssm, state_ssm_conv, state_lru, state_lru_conv, w_mod, b_mod, g_pre_mix, g_post_mix, g_pre_mlp, g_post_mlp, w_in, ssd_conv_w, ssd_conv_b, ssd_dt_bias, ssd_a_log, ssd_d, ssd_norm_g, lru_conv_w, lru_conv_b, lru_wa, lru_ba, lru_wx, lru_bx, lru_lambda, w_branch_ssd, w_branch_moba, w_branch_lru, w_out, w_up, w_down):
    bp, tp, d = x_prompt.shape
    bs, ts, _ = x_sample.shape
    depth = w_in.shape[0]
    n_pool = cache_k.shape[1]
    assert ts <= SAMPLE_T_PAD and ts >= SSD_CONV - 1
    assert (page_table.shape[1] * PAGE_SIZE) % MOBA_BLOCK == 0
    assert tp % SSD_CHUNK == 0 and tp % MOBA_BLOCK == 0

    tm_p = min(1024, tp)
    tm_dense_p = min(512, tp)
    lru_tm_p = min(512, tp)

    n_c = bp + bs
    n_c_pad = -(-n_c // SUBLANES) * SUBLANES
    c_all = jnp.pad(jnp.concatenate([c_prompt, c_sample], axis=0), ((0, n_c_pad - n_c), (0, 0)))
    mod_all = _mod_call(c_all, w_mod, b_mod)

    xs_pad = jnp.pad(x_sample, ((0, 0), (0, SAMPLE_T_PAD - ts), (0, 0)))

    zero_ssd_buf = jnp.zeros((bp, SSD_CONV - 1, SSD_CONV_DIM), F32)
    zero_ssd_h = jnp.zeros((1, bp, SSD_N_HEADS, SSD_HEAD_DIM, SSD_D_STATE), F32)
    zero_lru_buf = jnp.zeros((bp, LRU_CONV - 1, LRU_WIDTH), F32)
    zero_lru_h = jnp.zeros((bp, LRU_WIDTH), F32)

    outs_p = [[] for _ in range(6)]
    outs_s = [[] for _ in range(6)]
    xp, xs = x_prompt, xs_pad
    for l in range(depth):
        prm = {
            'g_pre_mix': g_pre_mix[l], 'g_post_mix': g_post_mix[l], 'g_pre_mlp': g_pre_mlp[l],
            'g_post_mlp': g_post_mlp[l], 'ssd_conv_w': ssd_conv_w[l], 'ssd_conv_b': ssd_conv_b[l],
            'ssd_dt_bias': ssd_dt_bias[l], 'ssd_a_log': ssd_a_log[l], 'ssd_d': ssd_d[l],
            'ssd_norm_g': ssd_norm_g[l], 'lru_conv_w': lru_conv_w[l], 'lru_conv_b': lru_conv_b[l],
            'lru_ba': lru_ba[l].reshape(-1), 'lru_bx': lru_bx[l].reshape(-1),
            'lru_lambda': lru_lambda[l],
        }
        wts = _layer_weights(l, w_in, w_branch_ssd, w_branch_moba, w_branch_lru, w_out, w_up,
                             w_down, lru_wa, lru_wx)
        mod_l = mod_all[l]
        mods_p = mod_l[:bp].reshape(bp, 1, 6 * d)
        mods_s = jnp.repeat(mod_l[bp:bp + bs], SAMPLE_T_PAD, axis=0).reshape(1, bs * SAMPLE_T_PAD, 6 * d)

        ksum_box = []

        def moba_prompt(p3, k3, v3, l=l, ksum_box=ksum_box):
            y, ksum = _moba_prompt_call(p3, k3, v3, page_table, cache_k, l)
            ksum_box.append(ksum)
            return y

        res_p = _group_layer(xp, mods_p, wts, prm, zero_ssd_buf, zero_ssd_h, 0, zero_lru_buf, zero_lru_h,
                             moba_prompt, t_valid=tp, tm_proj=tm_p, tm_dense=tm_dense_p,
                             ssd_rows=SSD_CHUNK, ssd_seqs=1, lru_tm=lru_tm_p, lru_seqs=1,
                             rows_per_mod=tp)
        xp = res_p[0]
        for n in range(6):
            outs_p[n].append(res_p[n + 1])

        def moba_sample(p3, k3, v3, l=l, ksum_box=ksum_box):
            sel = _select_call(p3, ksum_box[0])
            sel = sel[:, :ts, :MOBA_N_HEADS * 4].reshape(bs, ts, MOBA_N_HEADS, 4)[..., :MOBA_TOPK]
            return _moba_sample_call(page_table, sel, p3, k3, v3, cache_k, cache_v, l, t_valid=ts)

        res_s = _group_layer(xs, mods_s, wts, prm, state_ssm_conv[l], state_ssm, l, state_lru_conv[l],
                             state_lru[l], moba_sample, t_valid=ts, tm_proj=bs * SAMPLE_T_PAD,
                             tm_dense=bs * SAMPLE_T_PAD, ssd_rows=SAMPLE_T_PAD, ssd_seqs=2,
                             lru_tm=SAMPLE_T_PAD, lru_seqs=4,
                             rows_per_mod=bs * SAMPLE_T_PAD)
        xs = res_s[0]
        for n in range(6):
            outs_s[n].append(res_s[n + 1])

    st = lambda lst: jnp.stack(lst)
    kp, vp, ssmc_p, ssm_p, lruc_p, lru_p = outs_p
    ks, vs, ssmc_s, ssm_s, lruc_s, lru_s = outs_s
    return (xp, xs[:, :ts],
            st(kp), st(vp), st(ks), st(vs),
            st(ssm_p), st(ssmc_p), st(lru_p), st(lruc_p),
            st(ssm_s), st(ssmc_s), st(lru_s), st(lruc_s))
```
